```python
import math
import jax, jax.numpy as jnp
from jax import lax
import numpy as np

D_MODEL = 1024
BATCH = 2
SEQ = 8192
DEPTH = 4
DEC_BATCH = 16
DEC_SEQ = 16
PAST_LEN = 4096

CHUNK = 64
EPS = 1e-6
D_FF = 2816
CONV_W = 4
LRU_WIDTH = 1024
LRU_BLOCKS = 16
LRU_BS = LRU_WIDTH // LRU_BLOCKS
LRU_C = 8.0
SSD_HEADS = 16
SSD_HEAD_DIM = 64
SSD_INNER = SSD_HEADS * SSD_HEAD_DIM
SSD_GROUPS = 2
SSD_STATE = 128
SSD_CONV_DIM = SSD_INNER + 2 * SSD_GROUPS * SSD_STATE
N_HEADS = 16
N_KV_HEADS = 4
GQA = N_HEADS // N_KV_HEADS
HEAD_DIM = 64
ATT_WIDTH = N_HEADS * HEAD_DIM
KV_WIDTH = N_KV_HEADS * HEAD_DIM
WINDOW = 128
WIN_CHUNKS = WINDOW // CHUNK
BAND = (WIN_CHUNKS + 1) * CHUNK
NEG = -1e30
N_BRANCH = 3
IN_SIZES = (LRU_WIDTH, LRU_WIDTH, SSD_INNER, SSD_CONV_DIM, SSD_HEADS, ATT_WIDTH, KV_WIDTH, KV_WIDTH, N_BRANCH * D_MODEL)
N_IN = LRU_WIDTH * 2 + SSD_INNER + SSD_CONV_DIM + SSD_HEADS + ATT_WIDTH + 2 * KV_WIDTH + N_BRANCH * D_MODEL
BRANCH_WIDTH = LRU_WIDTH + SSD_INNER + ATT_WIDTH

kernel_name = 'hybrid_streaming_encoder_step'

f32 = jnp.float32


def rms_norm(x, g):
    x32 = x.astype(f32)
    y = x32 * lax.rsqrt(jnp.mean(x32 * x32, axis=-1, keepdims=True) + EPS)
    return (y * g.astype(f32)).astype(x.dtype)


def swiglu(x, wg, wu, wd):
    return (jax.nn.silu(x @ wg) * (x @ wu)) @ wd


def split_cols(h):
    idx = []
    acc = 0
    for s in IN_SIZES[:-1]:
        acc += s
        idx.append(acc)
    return jnp.split(h, idx, axis=-1)


def causal_conv(x, buf, w, b):
    xp = jnp.concatenate([buf, x], axis=1)
    L = x.shape[1]
    y = b + xp[:, 0:L] * w[0]
    for j in range(1, CONV_W):
        y = y + xp[:, j:j + L] * w[j]
    return y, xp[:, -(CONV_W - 1):]


def _lin_combine(e1, e2):
    a1, b1 = e1
    a2, b2 = e2
    return a1 * a2, a2 * b1 + b2


def rglru(x, h0, w_a, b_a, w_x, b_x, lam):
    bsz, L, W = x.shape
    xb = x.reshape(bsz, L, LRU_BLOCKS, LRU_BS)
    r = jax.nn.sigmoid(jnp.einsum('blnk,nkj->blnj', xb, w_a).reshape(bsz, L, W) + b_a)
    i = jax.nn.sigmoid(jnp.einsum('blnk,nkj->blnj', xb, w_x).reshape(bsz, L, W) + b_x)
    log_a = -LRU_C * r.astype(f32) * jax.nn.softplus(-lam.astype(f32))
    a = jnp.exp(log_a)
    u = jnp.sqrt(-jnp.expm1(2.0 * log_a)) * (i * x).astype(f32)
    a_cum, u_cum = lax.associative_scan(_lin_combine, (a, u), axis=1)
    h = a_cum * h0.astype(f32)[:, None] + u_cum
    return h.astype(x.dtype), h[:, -1].astype(x.dtype)


def ssd_scan(x, dt, A, B, C, h0, q):
    b, l, H, P = x.shape
    G, N = B.shape[2], B.shape[3]
    R = H // G
    nc = l // q
    x = x.reshape(b, nc, q, G, R, P)
    dt = dt.reshape(b, nc, q, G, R)
    B = B.reshape(b, nc, q, G, N)
    C = C.reshape(b, nc, q, G, N)
    cum = jnp.cumsum(dt * A.reshape(G, R), axis=2)
    seg = cum[:, :, :, None] - cum[:, :, None, :]
    causal = jnp.tril(jnp.ones((q, q), bool))[:, :, None, None]
    Lmat = jnp.where(causal, jnp.exp(jnp.where(causal, seg, 0.0)), 0.0)
    CB = jnp.einsum('bcign,bcjgn->bcijg', C, B)
    Wm = CB[..., None] * Lmat * dt[:, :, None]
    y_diag = jnp.einsum('bcijgr,bcjgrp->bcigrp', Wm, x)
    decay_end = jnp.exp(cum[:, :, -1:] - cum) * dt
    states = jnp.einsum('bcjgn,bcjgr,bcjgrp->bcgrpn', B, decay_end, x)
    chunk_decay = jnp.exp(cum[:, :, -1])

    def step(h, inp):
        dec, st = inp
        return dec[..., None, None] * h + st, h

    h_fin, h_in = lax.scan(step, h0.reshape(b, G, R, P, N),
                           (jnp.moveaxis(chunk_decay, 1, 0), jnp.moveaxis(states, 1, 0)))
    h_in = jnp.moveaxis(h_in, 0, 1)
    y_off = jnp.einsum('bcign,bcigr,bcgrpn->bcigrp', C, jnp.exp(cum), h_in)
    return (y_diag + y_off).reshape(b, l, H, P), h_fin.reshape(b, H, P, N)


def ssd_branch(z, xbc, dt_raw, conv_buf, h0, conv_w, conv_b, dt_bias, a_log, d_skip, norm_g, q):
    bsz, L = z.shape[:2]
    xbc, new_buf = causal_conv(xbc, conv_buf, conv_w, conv_b)
    xbc = jax.nn.silu(xbc)
    xs, Bm, Cm = jnp.split(xbc, [SSD_INNER, SSD_INNER + SSD_GROUPS * SSD_STATE], axis=-1)
    xs = xs.reshape(bsz, L, SSD_HEADS, SSD_HEAD_DIM).astype(f32)
    Bm = Bm.reshape(bsz, L, SSD_GROUPS, SSD_STATE).astype(f32)
    Cm = Cm.reshape(bsz, L, SSD_GROUPS, SSD_STATE).astype(f32)
    dt = jax.nn.softplus(dt_raw.astype(f32) + dt_bias.astype(f32))
    A = -jnp.exp(a_log.astype(f32))
    y, h = ssd_scan(xs, dt, A, Bm, Cm, h0.astype(f32), q)
    y = y + d_skip.astype(f32)[:, None] * xs
    y = (y.reshape(bsz, L, SSD_INNER) * jax.nn.silu(z.astype(f32))).astype(z.dtype)
    gs = SSD_INNER // SSD_GROUPS
    y = rms_norm(y.reshape(bsz, L, SSD_GROUPS, gs), norm_g.reshape(SSD_GROUPS, gs)).reshape(bsz, L, SSD_INNER)
    return y, new_buf, h.astype(z.dtype)


def alibi_slopes():
    return jnp.exp2(-8.0 * (jnp.arange(N_HEADS, dtype=f32) + 1.0) / N_HEADS)


def attend(q, k, v, q_pos, k_pos, sink):
    s = jnp.einsum('bnqkgd,bnskd->bnkgqs', q.astype(f32), k.astype(f32)) * (HEAD_DIM ** -0.5)
    dist = jnp.abs(q_pos[:, :, None] - k_pos[:, None, :]).astype(f32)
    slopes = alibi_slopes().reshape(N_KV_HEADS, GQA)
    s = s - slopes[None, None, :, :, None, None] * dist[None, :, None, None]
    qc = q_pos[:, :, None] // CHUNK
    kc = k_pos[:, None, :] // CHUNK
    visible = (k_pos[:, None, :] >= 0) & (kc >= qc - WIN_CHUNKS) & (kc <= qc)
    s = jnp.where(visible[None, :, None, None], s, NEG)
    sk = sink.astype(f32).reshape(N_KV_HEADS, GQA)[None, None, :, :, None, None]
    m = jnp.maximum(jnp.max(s, axis=-1, keepdims=True), sk)
    p = jnp.exp(s - m)
    den = jnp.sum(p, axis=-1, keepdims=True) + jnp.exp(sk - m)
    return jnp.einsum('bnkgqs,bnskd->bnqkgd', p / den, v.astype(f32))


def swa_prompt(q, k, v, sink):
    bsz, L = q.shape[:2]
    nc = L // CHUNK
    pad = WIN_CHUNKS * CHUNK
    qg = q.reshape(bsz, nc, CHUNK, N_KV_HEADS, GQA, HEAD_DIM)

    def band(t):
        tp = jnp.pad(t, ((0, 0), (pad, 0), (0, 0), (0, 0))).reshape(bsz, nc + WIN_CHUNKS, CHUNK, N_KV_HEADS, HEAD_DIM)
        return jnp.concatenate([tp[:, j:j + nc] for j in range(WIN_CHUNKS + 1)], axis=2)

    q_pos = jnp.arange(nc)[:, None] * CHUNK + jnp.arange(CHUNK)[None]
    k_pos = (jnp.arange(nc)[:, None] - WIN_CHUNKS) * CHUNK + jnp.arange(BAND)[None]
    o = attend(qg, band(k), band(v), q_pos, k_pos, sink)
    return o.reshape(bsz, L, ATT_WIDTH).astype(q.dtype), k[:, -WINDOW:], v[:, -WINDOW:]


def swa_sample(q, k, v, cache_k, cache_v, sink):
    bsz, L = q.shape[:2]
    n_cache = cache_k.shape[1]
    kk = jnp.concatenate([cache_k, k], axis=1)
    vv = jnp.concatenate([cache_v, v], axis=1)
    qg = q.reshape(bsz, 1, L, N_KV_HEADS, GQA, HEAD_DIM)
    q_pos = (PAST_LEN + jnp.arange(L))[None]
    k_pos = (PAST_LEN - n_cache + jnp.arange(n_cache + L))[None]
    o = attend(qg, kk[:, None], vv[:, None], q_pos, k_pos, sink)
    return o.reshape(bsz, L, ATT_WIDTH).astype(q.dtype), kk[:, -n_cache:], vv[:, -n_cache:]


def token_mixer(n, p, l, st, prompt):
    bsz, L = n.shape[:2]
    cache_k, cache_v, lru_conv, lru_h, ssd_conv, ssd_h = st
    x_lru, g_lru, z, xbc, dt_raw, q, k, v, g_pre = split_cols(n @ p['w_in'][l])
    xa, new_lru_conv = causal_conv(x_lru, lru_conv, p['lru_conv_w'][l], p['lru_conv_b'][l])
    ha, new_lru_h = rglru(xa, lru_h, p['lru_w_a'][l], p['lru_b_a'][l], p['lru_w_x'][l], p['lru_b_x'][l], p['lru_lambda'][l])
    y_a = ha * jax.nn.gelu(g_lru)
    y_b, new_ssd_conv, new_ssd_h = ssd_branch(z, xbc, dt_raw, ssd_conv, ssd_h, p['ssd_conv_w'][l], p['ssd_conv_b'][l],
                                              p['ssd_dt_bias'][l], p['ssd_a_log'][l], p['ssd_d'][l], p['ssd_norm'][l],
                                              CHUNK if prompt else L)
    q = q.reshape(bsz, L, N_HEADS, HEAD_DIM)
    k = k.reshape(bsz, L, N_KV_HEADS, HEAD_DIM)
    v = v.reshape(bsz, L, N_KV_HEADS, HEAD_DIM)
    if prompt:
        y_c, new_k, new_v = swa_prompt(q, k, v, p['attn_sink'][l])
    else:
        y_c, new_k, new_v = swa_sample(q, k, v, cache_k, cache_v, p['attn_sink'][l])
    gates = jax.nn.sigmoid(g_pre).reshape(bsz, L, N_BRANCH, D_MODEL)
    wb = p['w_branch'][l]
    merged = (gates[:, :, 0] * (y_a @ wb[:LRU_WIDTH])
              + gates[:, :, 1] * (y_b @ wb[LRU_WIDTH:LRU_WIDTH + SSD_INNER])
              + gates[:, :, 2] * (y_c @ wb[LRU_WIDTH + SSD_INNER:]))
    return merged @ p['w_out'][l], (new_k, new_v, new_lru_conv, new_lru_h, new_ssd_conv, new_ssd_h)


def run_layer(x, p, l, st, prompt):
    x = x + 0.5 * swiglu(rms_norm(x, p['ffn_norm'][l, 0]), p['ffn_w_gate'][l, 0], p['ffn_w_up'][l, 0], p['ffn_w_down'][l, 0])
    mix, new_st = token_mixer(rms_norm(x, p['mix_norm'][l]), p, l, st, prompt)
    x = x + mix
    x = x + 0.5 * swiglu(rms_norm(x, p['ffn_norm'][l, 1]), p['ffn_w_gate'][l, 1], p['ffn_w_up'][l, 1], p['ffn_w_down'][l, 1])
    return x, new_st


def setup_inputs(seed: int = 0) -> dict:
    key = jax.random.key(seed)
    ks = iter(jax.random.split(key, 40))

    def nrm(shape, scale):
        return scale * jax.random.normal(next(ks), shape, f32)

    L = DEPTH
    att_rows = min(WINDOW, PAST_LEN)
    u = jax.random.uniform(next(ks), (L, LRU_WIDTH), f32, 0.9, 0.999)
    s = u ** (1.0 / LRU_C)
    lru_lambda = jnp.log(s) - jnp.log1p(-s)
    dt = jnp.exp(jax.random.uniform(next(ks), (L, SSD_HEADS), f32, math.log(1e-3), math.log(1e-1)))
    ssd_dt_bias = dt + jnp.log(-jnp.expm1(-dt))
    ssd_a_log = jnp.log(jax.random.uniform(next(ks), (L, SSD_HEADS), f32, 1.0, 16.0))
    return {
        'x_prompt': nrm((BATCH, SEQ, D_MODEL), 1.0),
        'x_sample': nrm((DEC_BATCH, DEC_SEQ, D_MODEL), 1.0),
        'cache_attn_k': nrm((L, DEC_BATCH, att_rows, N_KV_HEADS, HEAD_DIM), 1.0),
        'cache_attn_v': nrm((L, DEC_BATCH, att_rows, N_KV_HEADS, HEAD_DIM), 1.0),
        'state_lru_conv': nrm((L, DEC_BATCH, CONV_W - 1, LRU_WIDTH), 1.0),
        'state_lru_h': nrm((L, DEC_BATCH, LRU_WIDTH), 0.5),
        'state_ssd_conv': nrm((L, DEC_BATCH, CONV_W - 1, SSD_CONV_DIM), 1.0),
        'state_ssd_h': nrm((L, DEC_BATCH, SSD_HEADS, SSD_HEAD_DIM, SSD_STATE), 0.1),
        'ffn_norm': 1.0 + nrm((L, 2, D_MODEL), 0.01),
        'ffn_w_gate': nrm((L, 2, D_MODEL, D_FF), D_MODEL ** -0.5),
        'ffn_w_up': nrm((L, 2, D_MODEL, D_FF), D_MODEL ** -0.5),
        'ffn_w_down': nrm((L, 2, D_FF, D_MODEL), D_FF ** -0.5),
        'mix_norm': 1.0 + nrm((L, D_MODEL), 0.01),
        'w_in': nrm((L, D_MODEL, N_IN), D_MODEL ** -0.5),
        'lru_conv_w': nrm((L, CONV_W, LRU_WIDTH), CONV_W ** -0.5),
        'lru_conv_b': nrm((L, LRU_WIDTH), 0.01),
        'lru_w_a': nrm((L, LRU_BLOCKS, LRU_BS, LRU_BS), LRU_BS ** -0.5),
        'lru_b_a': nrm((L, LRU_WIDTH), 0.01),
        'lru_w_x': nrm((L, LRU_BLOCKS, LRU_BS, LRU_BS), LRU_BS ** -0.5),
        'lru_b_x': nrm((L, LRU_WIDTH), 0.01),
        'lru_lambda': lru_lambda,
        'ssd_conv_w': nrm((L, CONV_W, SSD_CONV_DIM), CONV_W ** -0.5),
        'ssd_conv_b': nrm((L, SSD_CONV_DIM), 0.01),
        'ssd_dt_bias': ssd_dt_bias,
        'ssd_a_log': ssd_a_log,
        'ssd_d': 1.0 + nrm((L, SSD_HEADS), 0.01),
        'ssd_norm': 1.0 + nrm((L, SSD_INNER), 0.01),
        'attn_sink': nrm((L, N_HEADS), 0.5),
        'w_branch': nrm((L, BRANCH_WIDTH, D_MODEL), 1024 ** -0.5),
        'w_out': nrm((L, D_MODEL, D_MODEL), D_MODEL ** -0.5),
        'final_norm': 1.0 + nrm((D_MODEL,), 0.01),
    }


def reference(x_prompt, x_sample, cache_attn_k, cache_attn_v, state_lru_conv, state_lru_h, state_ssd_conv, state_ssd_h,
              ffn_norm, ffn_w_gate, ffn_w_up, ffn_w_down, mix_norm, w_in, lru_conv_w, lru_conv_b, lru_w_a, lru_b_a,
              lru_w_x, lru_b_x, lru_lambda, ssd_conv_w, ssd_conv_b, ssd_dt_bias, ssd_a_log, ssd_d, ssd_norm, attn_sink,
              w_branch, w_out, final_norm):
    p = {'ffn_norm': ffn_norm, 'ffn_w_gate': ffn_w_gate, 'ffn_w_up': ffn_w_up, 'ffn_w_down': ffn_w_down,
         'mix_norm': mix_norm, 'w_in': w_in, 'lru_conv_w': lru_conv_w, 'lru_conv_b': lru_conv_b,
         'lru_w_a': lru_w_a, 'lru_b_a': lru_b_a, 'lru_w_x': lru_w_x, 'lru_b_x': lru_b_x, 'lru_lambda': lru_lambda,
         'ssd_conv_w': ssd_conv_w, 'ssd_conv_b': ssd_conv_b, 'ssd_dt_bias': ssd_dt_bias, 'ssd_a_log': ssd_a_log,
         'ssd_d': ssd_d, 'ssd_norm': ssd_norm, 'attn_sink': attn_sink, 'w_branch': w_branch, 'w_out': w_out}
    dt_ = x_prompt.dtype
    bp = x_prompt.shape[0]
    xp, xs = x_prompt, x_sample
    p_new = [[] for _ in range(6)]
    s_new = [[] for _ in range(6)]
    for l in range(DEPTH):
        st_p = (None, None,
                jnp.zeros((bp, CONV_W - 1, LRU_WIDTH), dt_), jnp.zeros((bp, LRU_WIDTH), dt_),
                jnp.zeros((bp, CONV_W - 1, SSD_CONV_DIM), dt_),
                jnp.zeros((bp, SSD_HEADS, SSD_HEAD_DIM, SSD_STATE), dt_))
        xp, np_st = run_layer(xp, p, l, st_p, True)
        st_s = (cache_attn_k[l], cache_attn_v[l], state_lru_conv[l], state_lru_h[l], state_ssd_conv[l], state_ssd_h[l])
        xs, ns_st = run_layer(xs, p, l, st_s, False)
        for j in range(6):
            p_new[j].append(np_st[j])
            s_new[j].append(ns_st[j])
    y_prompt = rms_norm(xp, final_norm)
    y_sample = rms_norm(xs, final_norm)
    pk, pv, plc, plh, psc, psh = [jnp.stack(t, axis=0) for t in p_new]
    sk, sv, slc, slh, ssc, ssh = [jnp.stack(t, axis=0) for t in s_new]
    return (y_prompt, y_sample, pk, pv, plc, plh, psc, psh, sk, sv, slc, slh, ssc, ssh)
```

```python
import functools
import math

import jax
import jax.numpy as jnp
from jax import lax
from jax.experimental import pallas as pl
from jax.experimental.pallas import tpu as pltpu

f32 = jnp.float32
bf16 = jnp.bfloat16

EPS = 1e-6
CHUNK = 64
CONV_W = 4
LRU_BLOCKS = 16
LRU_C = 8.0
LRU_GROUP = 256
SSD_HEADS = 16
SSD_HEAD_DIM = 64
SSD_GROUPS = 2
SSD_STATE = 128
N_HEADS = 16
N_KV_HEADS = 4
HEAD_DIM = 64
WINDOW = 128
WIN_CHUNKS = WINDOW // CHUNK
PAST_LEN = 4096
NEG = -1e30
LANES = 128
SUBLANES = 8
VMEM_LIMIT = 56 * 1024 * 1024

_HI = lax.Precision.HIGHEST


def _rms(x, g):
    return x * lax.rsqrt(jnp.mean(x * x, axis=-1, keepdims=True) + EPS) * g


def _silu(x):
    return x * jax.nn.sigmoid(x)


def _softplus(x):
    return jnp.maximum(x, 0.0) + jnp.log1p(jnp.exp(-jnp.abs(x)))


def _gelu_tanh(x):
    c = math.sqrt(2.0 / math.pi)
    return x * (0.5 * (1.0 + jnp.tanh(c * (x + 0.044715 * (x * x * x)))))


def _dot(a, b):
    return jnp.dot(a, b, preferred_element_type=f32)


def _dot_nt(a, b):
    return lax.dot_general(a, b, (((1,), (1,)), ((), ())), preferred_element_type=f32)


def _dot_tn(a, b):
    return lax.dot_general(a, b, (((0,), (0,)), ((), ())), preferred_element_type=f32)


def _causal_conv(tail, x, cw, cb):
    rows = x.shape[0]
    ext = jnp.concatenate([tail, x], axis=0)
    y = cb + cw[CONV_W - 1:CONV_W, :] * x
    for s in range(1, CONV_W):
        y = y + cw[CONV_W - 1 - s:CONV_W - s, :] * pltpu.roll(ext, s, 0)[SUBLANES:SUBLANES + rows]
    return y


def _resident(shape, index_map):
    return pl.BlockSpec(shape, index_map, pipeline_mode=pl.Buffered(1))


def _ffn_kernel(*refs, n_chunks, cf, final):
    if final:
        x_ref, g_ref, wg_ref, wu_ref, wd_ref, fg_ref, o_ref, a_ref = refs
    else:
        x_ref, g_ref, wg_ref, wu_ref, wd_ref, o_ref, a_ref = refs
    x = x_ref[...]
    n = _rms(x, g_ref[...]).astype(bf16)
    for c in range(n_chunks):
        sl = slice(c * cf, (c + 1) * cf)
        gate = _dot(n, wg_ref[:, sl])
        up = _dot(n, wu_ref[:, sl])
        a_ref[:, sl] = (_silu(gate) * up).astype(bf16)
    y = x + 0.5 * _dot(a_ref[...], wd_ref[...])
    if final:
        y = _rms(y, fg_ref[...])
    o_ref[...] = y


def _ffn(x, norm, wg, wu, wd, l, j, final_norm=None):
    T, D = x.shape
    F = wg.shape[-1]
    tm = min(T, 512)
    cf = 256
    final = final_norm is not None
    wmap = lambda i: (l, j, 0, 0)
    in_specs = [
        pl.BlockSpec((tm, D), lambda i: (i, 0)),
        _resident((None, None, 1, D), wmap),
        _resident((None, None, D, F), wmap),
        _resident((None, None, D, F), wmap),
        _resident((None, None, F, D), wmap),
    ]
    args = [x, norm, wg, wu, wd]
    if final:
        in_specs.append(_resident((1, D), lambda i: (0, 0)))
        args.append(final_norm)
    return pl.pallas_call(
        functools.partial(_ffn_kernel, n_chunks=F // cf, cf=cf, final=final),
        grid=(T // tm,),
        in_specs=in_specs,
        out_specs=pl.BlockSpec((tm, D), lambda i: (i, 0)),
        out_shape=jax.ShapeDtypeStruct((T, D), f32),
        scratch_shapes=[pltpu.VMEM((tm, F), bf16)],
        compiler_params=pltpu.CompilerParams(dimension_semantics=("parallel",), vmem_limit_bytes=VMEM_LIMIT),
        name="ffn_final" if final else "ffn",
    )(*args)


def _lru_kernel(x_ref, g_ref, w_ref, cw_ref, cb_ref, wax_ref, ba_ref, bx_ref, lam_ref, conv0_ref, h0_ref,
                y_ref, convo_ref, ho_ref, tail_ref, hc_ref, a_scr, u_scr, *, width):
    W = width
    rows = x_ref.shape[0]

    @pl.when(pl.program_id(1) == 0)
    def _():
        tail_ref[...] = conv0_ref[...]
        hc_ref[...] = h0_ref[...]

    n = _rms(x_ref[...], g_ref[...]).astype(bf16)
    p = _dot(n, w_ref[...])
    xl = p[:, :W]
    gl = p[:, W:]
    xa = _causal_conv(tail_ref[...], xl, cw_ref[...], cb_ref[...])
    new_tail = xl[rows - SUBLANES:, :]
    tail_ref[...] = new_tail
    convo_ref[...] = new_tail

    xab = xa.astype(bf16)
    ng = W // LRU_GROUP
    parts = [_dot(xab[:, k * LRU_GROUP:(k + 1) * LRU_GROUP], wax_ref[k]) for k in range(ng)]
    r = jax.nn.sigmoid(jnp.concatenate([q[:, :LRU_GROUP] for q in parts], axis=1) + ba_ref[...])
    i = jax.nn.sigmoid(jnp.concatenate([q[:, LRU_GROUP:] for q in parts], axis=1) + bx_ref[...])
    log_a = (-LRU_C) * r * _softplus(-lam_ref[...])
    a = jnp.exp(log_a)
    th = jnp.tanh(log_a)
    u = jnp.sqrt(-2.0 * th / (1.0 - th)) * (i * xa)

    sub = lax.broadcasted_iota(jnp.int32, (rows, W), 0) & (SUBLANES - 1)
    for s in (1, 2, 4):
        keep = sub >= s
        u = jnp.where(keep, a * pltpu.roll(u, s, 0) + u, u)
        a = jnp.where(keep, a * pltpu.roll(a, s, 0), a)
    a_scr[...] = a
    u_scr[...] = u

    def group(k, h):
        off = pl.multiple_of(k * SUBLANES, SUBLANES)
        hg = a_scr[pl.ds(off, SUBLANES), :] * h + u_scr[pl.ds(off, SUBLANES), :]
        u_scr[pl.ds(off, SUBLANES), :] = hg
        return hg[SUBLANES - 1:SUBLANES, :]

    h_last = lax.fori_loop(0, rows // SUBLANES, group, hc_ref[...])
    hc_ref[...] = h_last
    ho_ref[...] = h_last
    y_ref[...] = u_scr[...] * _gelu_tanh(gl)


def _lru(x, wts, l, conv8, h0, tl):
    B, L, D = x.shape
    W = conv8.shape[-1]
    lmap = lambda b, t: (l, 0, 0)
    return pl.pallas_call(
        functools.partial(_lru_kernel, width=W),
        grid=(B, L // tl),
        in_specs=[
            pl.BlockSpec((None, tl, D), lambda b, t: (b, t, 0)),
            _resident((None, 1, D), lmap),
            _resident((None, D, 2 * W), lmap),
            _resident((None, CONV_W, W), lmap),
            _resident((None, 1, W), lmap),
            _resident((None, W // LRU_GROUP, LRU_GROUP, 2 * LRU_GROUP), lambda b, t: (l, 0, 0, 0)),
            _resident((None, 1, W), lmap),
            _resident((None, 1, W), lmap),
            _resident((None, 1, W), lmap),
            pl.BlockSpec((None, SUBLANES, W), lambda b, t: (b, 0, 0)),
            pl.BlockSpec((None, 1, W), lambda b, t: (b, 0, 0)),
        ],
        out_specs=[
            pl.BlockSpec((None, tl, W), lambda b, t: (b, t, 0)),
            pl.BlockSpec((None, SUBLANES, W), lambda b, t: (b, 0, 0)),
            pl.BlockSpec((None, 1, W), lambda b, t: (b, 0, 0)),
        ],
        out_shape=[
            jax.ShapeDtypeStruct((B, L, W), f32),
            jax.ShapeDtypeStruct((B, SUBLANES, W), f32),
            jax.ShapeDtypeStruct((B, 1, W), f32),
        ],
        scratch_shapes=[
            pltpu.VMEM((SUBLANES, W), f32),
            pltpu.VMEM((1, W), f32),
            pltpu.VMEM((tl, W), f32),
            pltpu.VMEM((tl, W), f32),
        ],
        compiler_params=pltpu.CompilerParams(dimension_semantics=("arbitrary", "arbitrary"),
                                             vmem_limit_bytes=VMEM_LIMIT),
        name="lru",
    )(x, wts["mix_norm"], wts["w_lru"], wts["lru_conv_w"], wts["lru_conv_b"], wts["lru_wax"], wts["lru_b_a"],
      wts["lru_b_x"], wts["lru_lambda"], conv8, h0)


def _ssd_kernel(x_ref, g_ref, wzx_ref, wdt_ref, wdtt_ref, cw_ref, cb_ref, dtb_ref, dtbt_ref, alog_ref, alogt_ref,
                dskip_ref, ng_ref, expand_ref, conv0_ref, h0_ref,
                y_ref, convo_ref, ho_ref,
                tail_ref, hs_ref, n_scr, z_scr, xc_scr, y_scr, *, q):
    rows = x_ref.shape[0]
    inner = SSD_HEADS * SSD_HEAD_DIM
    gn = SSD_GROUPS * SSD_STATE
    hpg = SSD_HEADS // SSD_GROUPS
    gw = hpg * SSD_HEAD_DIM

    @pl.when(pl.program_id(1) == 0)
    def _():
        tail_ref[...] = conv0_ref[...]
        hs_ref[...] = h0_ref[...]

    n = _rms(x_ref[...], g_ref[...]).astype(bf16)
    n_scr[...] = n
    p = _dot(n, wzx_ref[...])
    z_scr[...] = p[:, :inner]
    xr = p[:, inner:]
    xc_scr[...] = _silu(_causal_conv(tail_ref[...], xr, cw_ref[...], cb_ref[...]))
    new_tail = xr[rows - SUBLANES:, :]
    tail_ref[...] = new_tail
    convo_ref[...] = new_tail

    ii = lax.broadcasted_iota(jnp.int32, (q, q), 0)
    jj = lax.broadcasted_iota(jnp.int32, (q, q), 1)
    causal = jj <= ii
    tri = causal.astype(f32)
    tri_t = (ii <= jj).astype(f32)
    lane_hi = lax.broadcasted_iota(jnp.int32, (q, LANES), 1) >= SSD_HEAD_DIM
    row_hi = lax.broadcasted_iota(jnp.int32, (LANES, SSD_STATE), 0) >= SSD_HEAD_DIM
    a_neg = -jnp.exp(alog_ref[...])
    a_neg_t = -jnp.exp(alogt_ref[...])

    def chunk(c, carry):
        r0 = pl.multiple_of(c * q, q)
        rs = pl.ds(r0, q)
        nc = n_scr[rs, :]
        dt = _softplus(_dot(nc, wdt_ref[...]) + dtb_ref[...])
        dt_t = _softplus(_dot_nt(wdtt_ref[...], nc) + dtbt_ref[...])
        cum = jnp.dot(tri, dt * a_neg, precision=_HI, preferred_element_type=f32)
        cum_t = jnp.dot(dt_t * a_neg_t, tri_t, precision=_HI, preferred_element_type=f32)
        last = cum[q - 1:q, :]
        decay_end = jnp.exp(last - cum) * dt
        chunk_decay_t = jnp.exp(cum_t[:, q - 1:q])

        xs = xc_scr[rs, 0:inner]
        bm = xc_scr[rs, inner:inner + gn]
        cm = xc_scr[rs, inner + gn:inner + 2 * gn]

        y_off = jnp.concatenate(
            [_dot_nt(cm[:, g * SSD_STATE:(g + 1) * SSD_STATE].astype(bf16),
                     hs_ref[g * gw:(g + 1) * gw, :].astype(bf16)) for g in range(SSD_GROUPS)], axis=1)
        ecum = jnp.dot(jnp.exp(cum), expand_ref[...], precision=_HI, preferred_element_type=f32)
        y_scr[rs, :] = y_off * ecum + dskip_ref[...] * xs

        for g in range(SSD_GROUPS):
            bg = bm[:, g * SSD_STATE:(g + 1) * SSD_STATE]
            cg = cm[:, g * SSD_STATE:(g + 1) * SSD_STATE]
            cb = _dot_nt(cg.astype(bf16), bg.astype(bf16))
            for pr in range(hpg // 2):
                m = g * (hpg // 2) + pr
                xp = xs[:, m * LANES:(m + 1) * LANES]
                y_pair = jnp.zeros((q, LANES), f32)
                s_pair = jnp.zeros((LANES, SSD_STATE), f32)
                for half in range(2):
                    h = 2 * m + half
                    seg = cum[:, h:h + 1] - cum_t[h:h + 1, :]
                    lmat = jnp.where(causal, jnp.exp(jnp.where(causal, seg, 0.0)), 0.0)
                    wm = cb * lmat * dt_t[h:h + 1, :]
                    xh = jnp.where(lane_hi, xp, 0.0) if half else jnp.where(lane_hi, 0.0, xp)
                    xhb = xh.astype(bf16)
                    y_pair = y_pair + _dot(wm.astype(bf16), xhb)
                    s_pair = s_pair + _dot_tn(xhb, (bg * decay_end[:, h:h + 1]).astype(bf16))
                y_scr[rs, m * LANES:(m + 1) * LANES] += y_pair
                scale = jnp.where(row_hi, chunk_decay_t[2 * m + 1:2 * m + 2, :], chunk_decay_t[2 * m:2 * m + 1, :])
                hs_ref[m * LANES:(m + 1) * LANES, :] = scale * hs_ref[m * LANES:(m + 1) * LANES, :] + s_pair
        return carry

    lax.fori_loop(0, rows // q, chunk, 0)
    ho_ref[...] = hs_ref[...]

    y = y_scr[...] * _silu(z_scr[...])
    ng = ng_ref[...]
    gs = inner // SSD_GROUPS
    for g in range(SSD_GROUPS):
        sl = slice(g * gs, (g + 1) * gs)
        y_ref[:, sl] = _rms(y[:, sl], ng[:, sl])


def _ssd(x, wts, l, conv8, h0, tq, q):
    B, L, D = x.shape
    inner = SSD_HEADS * SSD_HEAD_DIM
    cdim = inner + 2 * SSD_GROUPS * SSD_STATE
    lmap = lambda b, t: (l, 0, 0)
    return pl.pallas_call(
        functools.partial(_ssd_kernel, q=q),
        grid=(B, L // tq),
        in_specs=[
            pl.BlockSpec((None, tq, D), lambda b, t: (b, t, 0)),
            _resident((None, 1, D), lmap),
            _resident((None, D, inner + cdim), lmap),
            _resident((None, D, LANES), lmap),
            _resident((None, SSD_HEADS, D), lmap),
            _resident((None, CONV_W, cdim), lmap),
            _resident((None, 1, cdim), lmap),
            _resident((None, 1, LANES), lmap),
            _resident((None, SSD_HEADS, 1), lmap),
            _resident((None, 1, LANES), lmap),
            _resident((None, SSD_HEADS, 1), lmap),
            _resident((None, 1, inner), lmap),
            _resident((None, 1, inner), lmap),
            _resident((LANES, inner), lambda b, t: (0, 0)),
            pl.BlockSpec((None, SUBLANES, cdim), lambda b, t: (b, 0, 0)),
            pl.BlockSpec((None, inner, SSD_STATE), lambda b, t: (b, 0, 0)),
        ],
        out_specs=[
            pl.BlockSpec((None, tq, inner), lambda b, t: (b, t, 0)),
            pl.BlockSpec((None, SUBLANES, cdim), lambda b, t: (b, 0, 0)),
            pl.BlockSpec((None, inner, SSD_STATE), lambda b, t: (b, 0, 0)),
        ],
        out_shape=[
            jax.ShapeDtypeStruct((B, L, inner), f32),
            jax.ShapeDtypeStruct((B, SUBLANES, cdim), f32),
            jax.ShapeDtypeStruct((B, inner, SSD_STATE), f32),
        ],
        scratch_shapes=[
            pltpu.VMEM((SUBLANES, cdim), f32),
            pltpu.VMEM((inner, SSD_STATE), f32),
            pltpu.VMEM((tq, D), bf16),
            pltpu.VMEM((tq, inner), f32),
            pltpu.VMEM((tq, cdim), f32),
            pltpu.VMEM((tq, inner), f32),
        ],
        compiler_params=pltpu.CompilerParams(dimension_semantics=("arbitrary", "arbitrary"),
                                             vmem_limit_bytes=VMEM_LIMIT),
        name="ssd",
    )(x, wts["mix_norm"], wts["w_zx"], wts["w_dt"], wts["w_dt_t"], wts["ssd_conv_w"], wts["ssd_conv_b"],
      wts["ssd_dt_bias"], wts["ssd_dt_bias_t"], wts["ssd_a_log"], wts["ssd_a_log_t"], wts["ssd_d"],
      wts["ssd_norm"], wts["ssd_expand"], conv8, h0)


def _attn_kernel(slope_ref, sink_ref, x_ref, g_ref, w_ref, pk_ref, pv_ref,
                 y_ref, ko_ref, vo_ref, kp_scr, vp_scr, *, pos0):
    rows = x_ref.shape[0]
    s_len = WINDOW + rows
    aw = N_HEADS * HEAD_DIM
    kw = N_KV_HEADS * HEAD_DIM
    gqa = N_HEADS // N_KV_HEADS
    t = pl.program_id(1)

    @pl.when(t == 0)
    def _():
        kp_scr[...] = pk_ref[...]
        vp_scr[...] = pv_ref[...]

    n = _rms(x_ref[...], g_ref[...]).astype(bf16)
    p = _dot(n, w_ref[...])
    qq = p[:, :aw]
    kk = jnp.concatenate([kp_scr[...], p[:, aw:aw + kw]], axis=0)
    vv = jnp.concatenate([vp_scr[...], p[:, aw + kw:aw + 2 * kw]], axis=0)
    k_new = kk[rows:, :]
    v_new = vv[rows:, :]
    kp_scr[...] = k_new
    vp_scr[...] = v_new
    ko_ref[...] = k_new
    vo_ref[...] = v_new

    iq = lax.broadcasted_iota(jnp.int32, (rows, s_len), 0)
    ik = lax.broadcasted_iota(jnp.int32, (rows, s_len), 1)
    dist = jnp.abs(iq - (ik - WINDOW)).astype(f32)
    qc = iq // CHUNK
    kc = ik // CHUNK - WIN_CHUNKS
    k_pos = ik + (pos0 - WINDOW + t * rows)
    visible = (k_pos >= 0) & (kc >= qc - WIN_CHUNKS) & (kc <= qc)

    lane_q = lax.broadcasted_iota(jnp.int32, (rows, LANES), 1) >= HEAD_DIM
    lane_k = lax.broadcasted_iota(jnp.int32, (s_len, LANES), 1) >= HEAD_DIM
    scale = HEAD_DIM ** -0.5

    for kvh in range(N_KV_HEADS):
        tile = kvh // 2
        odd = kvh % 2 == 1
        kt = kk[:, tile * LANES:(tile + 1) * LANES]
        vt = vv[:, tile * LANES:(tile + 1) * LANES]
        k_own = jnp.where(lane_k, kt, 0.0) if odd else jnp.where(lane_k, 0.0, kt)
        v_own = jnp.where(lane_k, vt, 0.0) if odd else jnp.where(lane_k, 0.0, vt)
        k_both = (k_own + pltpu.roll(k_own, HEAD_DIM, 1)).astype(bf16)
        v_other = pltpu.roll(v_own, HEAD_DIM, 1)
        v_lo = (v_other if odd else v_own).astype(bf16)
        v_hi = (v_own if odd else v_other).astype(bf16)
        for pr in range(gqa // 2):
            m = kvh * (gqa // 2) + pr
            qp = qq[:, m * LANES:(m + 1) * LANES]
            o_pair = jnp.zeros((rows, LANES), f32)
            for half in range(2):
                h = 2 * m + half
                qh = (jnp.where(lane_q, qp, 0.0) if half else jnp.where(lane_q, 0.0, qp)).astype(bf16)
                s = _dot_nt(qh, k_both) * scale - slope_ref[h] * dist
                s = jnp.where(visible, s, NEG)
                mx = jnp.maximum(jnp.max(s, axis=-1, keepdims=True), sink_ref[h])
                e = jnp.exp(s - mx)
                den = jnp.sum(e, axis=-1, keepdims=True) + jnp.exp(sink_ref[h] - mx)
                o_pair = o_pair + _dot(e.astype(bf16), v_hi if half else v_lo) / den
            y_ref[:, m * LANES:(m + 1) * LANES] = o_pair


def _attn(x, wts, l, slopes, sink, prev_k, prev_v, tq, pos0):
    B, L, D = x.shape
    aw = N_HEADS * HEAD_DIM
    kw = N_KV_HEADS * HEAD_DIM
    lmap = lambda b, t: (l, 0, 0)
    smem = pl.BlockSpec(memory_space=pltpu.SMEM)
    return pl.pallas_call(
        functools.partial(_attn_kernel, pos0=pos0),
        grid=(B, L // tq),
        in_specs=[
            smem, smem,
            pl.BlockSpec((None, tq, D), lambda b, t: (b, t, 0)),
            _resident((None, 1, D), lmap),
            _resident((None, D, aw + 2 * kw), lmap),
            pl.BlockSpec((None, WINDOW, kw), lambda b, t: (b, 0, 0)),
            pl.BlockSpec((None, WINDOW, kw), lambda b, t: (b, 0, 0)),
        ],
        out_specs=[
            pl.BlockSpec((None, tq, aw), lambda b, t: (b, t, 0)),
            pl.BlockSpec((None, WINDOW, kw), lambda b, t: (b, 0, 0)),
            pl.BlockSpec((None, WINDOW, kw), lambda b, t: (b, 0, 0)),
        ],
        out_shape=[
            jax.ShapeDtypeStruct((B, L, aw), f32),
            jax.ShapeDtypeStruct((B, WINDOW, kw), f32),
            jax.ShapeDtypeStruct((B, WINDOW, kw), f32),
        ],
        scratch_shapes=[pltpu.VMEM((WINDOW, kw), f32), pltpu.VMEM((WINDOW, kw), f32)],
        compiler_params=pltpu.CompilerParams(dimension_semantics=("arbitrary", "arbitrary"),
                                             vmem_limit_bytes=VMEM_LIMIT),
        name="attn",
    )(slopes, sink, x, wts["mix_norm"], wts["w_qkv"], prev_k, prev_v)


def _merge_kernel(x_ref, ya_ref, yb_ref, yc_ref, g_ref, wg_ref, wb_ref, wo_ref, o_ref):
    x = x_ref[...]
    D = x.shape[-1]
    n = _rms(x, g_ref[...]).astype(bf16)
    merged = jnp.zeros(x.shape, f32)
    for b, y_ref in enumerate((ya_ref, yb_ref, yc_ref)):
        bw = y_ref.shape[-1]
        gate = jax.nn.sigmoid(_dot(n, wg_ref[:, b * D:(b + 1) * D]))
        merged = merged + gate * _dot(y_ref[...].astype(bf16), wb_ref[b * bw:(b + 1) * bw, :])
    o_ref[...] = x + _dot(merged.astype(bf16), wo_ref[...])


def _merge(x, ya, yb, yc, wts, l):
    T, D = x.shape
    tm = min(T, 512)
    lmap = lambda i: (l, 0, 0)
    tok = lambda w: pl.BlockSpec((tm, w), lambda i: (i, 0))
    return pl.pallas_call(
        _merge_kernel,
        grid=(T // tm,),
        in_specs=[
            tok(D), tok(ya.shape[-1]), tok(yb.shape[-1]), tok(yc.shape[-1]),
            _resident((None, 1, D), lmap),
            _resident((None, D, 3 * D), lmap),
            _resident((None, wts["w_branch"].shape[1], D), lmap),
            _resident((None, D, D), lmap),
        ],
        out_specs=tok(D),
        out_shape=jax.ShapeDtypeStruct((T, D), f32),
        compiler_params=pltpu.CompilerParams(dimension_semantics=("parallel",), vmem_limit_bytes=VMEM_LIMIT),
        name="merge",
    )(x, ya, yb, yc, wts["mix_norm"], wts["w_gates"], wts["w_branch"], wts["w_out"])


def _block_diag_groups(w):
    nl, nb, bs, _ = w.shape
    per = LRU_GROUP // bs
    w = w.reshape(nl, nb // per, per, bs, bs)
    eye = jnp.eye(per, dtype=w.dtype)
    out = w[:, :, :, :, None, :] * eye[None, None, :, None, :, None]
    return out.reshape(nl, nb // per, LRU_GROUP, LRU_GROUP)


def _prep_weights(ffn_norm, ffn_w_gate, ffn_w_up, ffn_w_down, mix_norm, w_in, lru_conv_w, lru_conv_b, lru_w_a,
                  lru_b_a, lru_w_x, lru_b_x, lru_lambda, ssd_conv_w, ssd_conv_b, ssd_dt_bias, ssd_a_log, ssd_d,
                  ssd_norm, w_branch, w_out):
    nl, D = mix_norm.shape
    W = lru_conv_w.shape[-1]
    inner = SSD_HEADS * SSD_HEAD_DIM
    cdim = ssd_conv_w.shape[-1]
    aw = N_HEADS * HEAD_DIM
    kw = N_KV_HEADS * HEAD_DIM
    o = [0]
    for s in (W, W, inner, cdim, SSD_HEADS, aw, kw, kw, 3 * D):
        o.append(o[-1] + s)
    row = lambda a: a.reshape(nl, 1, -1)
    pad_lanes = lambda a: jnp.pad(a, [(0, 0)] * (a.ndim - 1) + [(0, LANES - a.shape[-1])])
    w_dt = w_in[:, :, o[4]:o[5]]
    head_of_col = jnp.arange(inner) // SSD_HEAD_DIM
    return {
        "ffn_norm": ffn_norm.reshape(nl, 2, 1, D),
        "ffn_w_gate": ffn_w_gate.astype(bf16), "ffn_w_up": ffn_w_up.astype(bf16), "ffn_w_down": ffn_w_down.astype(bf16),
        "mix_norm": row(mix_norm),
        "w_lru": w_in[:, :, o[0]:o[2]].astype(bf16),
        "w_zx": w_in[:, :, o[2]:o[4]].astype(bf16),
        "w_dt": pad_lanes(w_dt).astype(bf16),
        "w_dt_t": jnp.swapaxes(w_dt, 1, 2).astype(bf16),
        "w_qkv": w_in[:, :, o[5]:o[8]].astype(bf16),
        "w_gates": w_in[:, :, o[8]:o[9]].astype(bf16),
        "lru_conv_w": lru_conv_w, "lru_conv_b": row(lru_conv_b),
        "lru_wax": jnp.concatenate([_block_diag_groups(lru_w_a), _block_diag_groups(lru_w_x)], axis=-1).astype(bf16),
        "lru_b_a": row(lru_b_a), "lru_b_x": row(lru_b_x), "lru_lambda": row(lru_lambda),
        "ssd_conv_w": ssd_conv_w, "ssd_conv_b": row(ssd_conv_b),
        "ssd_dt_bias": pad_lanes(row(ssd_dt_bias)), "ssd_dt_bias_t": ssd_dt_bias[:, :, None],
        "ssd_a_log": pad_lanes(row(ssd_a_log)), "ssd_a_log_t": ssd_a_log[:, :, None],
        "ssd_d": row(jnp.repeat(ssd_d, SSD_HEAD_DIM, axis=-1)),
        "ssd_norm": row(ssd_norm),
        "ssd_expand": (jnp.arange(LANES)[:, None] == head_of_col[None, :]).astype(f32),
        "w_branch": w_branch.astype(bf16), "w_out": w_out.astype(bf16),
    }


def _pad_conv_state(s):
    return jnp.pad(s, ((0, 0), (SUBLANES - s.shape[1], 0), (0, 0)))


def _layer(x, wts, l, slopes, sink, state, tiles, pos0, final_norm):
    B, L, D = x.shape
    cache_k, cache_v, lru_conv, lru_h, ssd_conv, ssd_h = state
    tl, tq_ssd, q, tq_att = tiles
    kw = N_KV_HEADS * HEAD_DIM
    inner = SSD_HEADS * SSD_HEAD_DIM
    ffn_w = (wts["ffn_norm"], wts["ffn_w_gate"], wts["ffn_w_up"], wts["ffn_w_down"])

    x = _ffn(x.reshape(B * L, D), *ffn_w, l, 0).reshape(B, L, D)
    ya, lru_conv8, lru_hn = _lru(x, wts, l, _pad_conv_state(lru_conv), lru_h[:, None, :], tl)
    yb, ssd_conv8, ssd_hn = _ssd(x, wts, l, _pad_conv_state(ssd_conv), ssd_h.reshape(B, inner, SSD_STATE), tq_ssd, q)
    yc, kn, vn = _attn(x, wts, l, slopes, sink, cache_k.reshape(B, WINDOW, kw), cache_v.reshape(B, WINDOW, kw),
                       tq_att, pos0)
    flat = lambda a: a.reshape(B * L, a.shape[-1])
    x = _merge(flat(x), flat(ya), flat(yb), flat(yc), wts, l)
    x = _ffn(x, *ffn_w, l, 1, final_norm=final_norm).reshape(B, L, D)
    new_state = (kn.reshape(B, WINDOW, N_KV_HEADS, HEAD_DIM), vn.reshape(B, WINDOW, N_KV_HEADS, HEAD_DIM),
                 lru_conv8[:, SUBLANES - (CONV_W - 1):], lru_hn[:, 0],
                 ssd_conv8[:, SUBLANES - (CONV_W - 1):], ssd_hn.reshape(B, SSD_HEADS, SSD_HEAD_DIM, SSD_STATE))
    return x, new_state


def kernel(x_prompt, x_sample, cache_attn_k, cache_attn_v, state_lru_conv, state_lru_h, state_ssd_conv, state_ssd_h, ffn_norm, ffn_w_gate, ffn_w_up, ffn_w_down, mix_norm, w_in, lru_conv_w, lru_conv_b, lru_w_a, lru_b_a, lru_w_x, lru_b_x, lru_lambda, ssd_conv_w, ssd_conv_b, ssd_dt_bias, ssd_a_log, ssd_d, ssd_norm, attn_sink, w_branch, w_out, final_norm):
    depth = mix_norm.shape[0]
    bp, lp, D = x_prompt.shape
    bs, ls, _ = x_sample.shape
    wts = _prep_weights(ffn_norm, ffn_w_gate, ffn_w_up, ffn_w_down, mix_norm, w_in, lru_conv_w, lru_conv_b, lru_w_a,
                        lru_b_a, lru_w_x, lru_b_x, lru_lambda, ssd_conv_w, ssd_conv_b, ssd_dt_bias, ssd_a_log, ssd_d,
                        ssd_norm, w_branch, w_out)
    slopes = jnp.exp2(-8.0 * (jnp.arange(N_HEADS, dtype=f32) + 1.0) / N_HEADS)
    fin = final_norm.reshape(1, D)
    kw = N_KV_HEADS * HEAD_DIM
    zero_state = (jnp.zeros((bp, WINDOW, kw), f32), jnp.zeros((bp, WINDOW, kw), f32),
                  jnp.zeros((bp, CONV_W - 1, lru_conv_w.shape[-1]), f32), jnp.zeros((bp, lru_conv_w.shape[-1]), f32),
                  jnp.zeros((bp, CONV_W - 1, ssd_conv_w.shape[-1]), f32),
                  jnp.zeros((bp, SSD_HEADS, SSD_HEAD_DIM, SSD_STATE), f32))
    tiles_p = (min(lp, 256), min(lp, 256), CHUNK, min(lp, 256))
    tiles_s = (ls, ls, ls, ls)
    xp, xs = x_prompt, x_sample
    p_new = [[] for _ in range(6)]
    s_new = [[] for _ in range(6)]
    for l in range(depth):
        last = fin if l == depth - 1 else None
        xp, st_p = _layer(xp, wts, l, slopes, attn_sink[l], zero_state, tiles_p, 0, last)
        st_s = (cache_attn_k[l], cache_attn_v[l], state_lru_conv[l], state_lru_h[l], state_ssd_conv[l], state_ssd_h[l])
        xs, st_sn = _layer(xs, wts, l, slopes, attn_sink[l], st_s, tiles_s, PAST_LEN, last)
        for j in range(6):
            p_new[j].append(st_p[j])
            s_new[j].append(st_sn[j])
    pk, pv, plc, plh, psc, psh = [jnp.stack(t, axis=0) for t in p_new]
    sk, sv, slc, slh, ssc, ssh = [jnp.stack(t, axis=0) for t in s_new]
    return (xp, xs, pk, pv, plc, plh, psc, psh, sk, sv, slc, slh, ssc, ssh)
```

```python
import functools
import math

import jax
import jax.numpy as jnp
from jax import lax
from jax.experimental import pallas as pl
from jax.experimental.pallas import tpu as pltpu

f32 = jnp.float32
bf16 = jnp.bfloat16

EPS = 1e-6
CHUNK = 64
CONV_W = 4
LRU_BLOCKS = 16
LRU_C = 8.0
LRU_GROUP = 256
SSD_HEADS = 16
SSD_HEAD_DIM = 64
SSD_GROUPS = 2
SSD_STATE = 128
N_HEADS = 16
N_KV_HEADS = 4
HEAD_DIM = 64
WINDOW = 128
WIN_CHUNKS = WINDOW // CHUNK
PAST_LEN = 4096
NEG = -1e30
MASK_DIST = -NEG * 2.0 ** 8
ATT_SUB = 128
SEQ_PER_STEP = 1
LANES = 128
SUBLANES = 8
VMEM_LIMIT = 56 * 1024 * 1024

_HI = lax.Precision.HIGHEST


def _rms(x, g):
    return x * lax.rsqrt(jnp.mean(x * x, axis=-1, keepdims=True) + EPS) * g


def _silu(x):
    return x * jax.nn.sigmoid(x)


def _softplus(x):
    return jnp.maximum(x, 0.0) + jnp.log1p(jnp.exp(-jnp.abs(x)))


def _gelu_tanh(x):
    c = math.sqrt(2.0 / math.pi)
    return x * (0.5 * (1.0 + jnp.tanh(c * (x + 0.044715 * (x * x * x)))))


def _dot(a, b):
    return jnp.dot(a, b, preferred_element_type=f32)


def _dot_nt(a, b):
    return lax.dot_general(a, b, (((1,), (1,)), ((), ())), preferred_element_type=f32)


def _dot_tn(a, b):
    return lax.dot_general(a, b, (((0,), (0,)), ((), ())), preferred_element_type=f32)


def _causal_conv(tail, x, cw, cb):
    rows = x.shape[0]
    ext = jnp.concatenate([tail, x], axis=0)
    y = cb + cw[CONV_W - 1:CONV_W, :] * x
    for s in range(1, CONV_W):
        y = y + cw[CONV_W - 1 - s:CONV_W - s, :] * pltpu.roll(ext, s, 0)[SUBLANES:SUBLANES + rows]
    return y


def _resident(shape, index_map):
    return pl.BlockSpec(shape, index_map, pipeline_mode=pl.Buffered(1))


def _ffn_kernel(*refs, n_chunks, cf, final):
    if final:
        x_ref, g_ref, wg_ref, wu_ref, wd_ref, fg_ref, o_ref, a_ref = refs
    else:
        x_ref, g_ref, wg_ref, wu_ref, wd_ref, o_ref, a_ref = refs
    x = x_ref[...]
    n = _rms(x, g_ref[...]).astype(bf16)
    for c in range(n_chunks):
        sl = slice(c * cf, (c + 1) * cf)
        gate = _dot(n, wg_ref[:, sl])
        up = _dot(n, wu_ref[:, sl])
        a_ref[:, sl] = (_silu(gate) * up).astype(bf16)
    y = x + 0.5 * _dot(a_ref[...], wd_ref[...])
    if final:
        y = _rms(y, fg_ref[...])
    o_ref[...] = y


def _ffn(x, norm, wg, wu, wd, l, j, final_norm=None):
    T, D = x.shape
    F = wg.shape[-1]
    tm = min(T, 512)
    cf = 256
    final = final_norm is not None
    wmap = lambda i: (l, j, 0, 0)
    in_specs = [
        pl.BlockSpec((tm, D), lambda i: (i, 0)),
        _resident((None, None, 1, D), wmap),
        _resident((None, None, D, F), wmap),
        _resident((None, None, D, F), wmap),
        _resident((None, None, F, D), wmap),
    ]
    args = [x, norm, wg, wu, wd]
    if final:
        in_specs.append(_resident((1, D), lambda i: (0, 0)))
        args.append(final_norm)
    return pl.pallas_call(
        functools.partial(_ffn_kernel, n_chunks=F // cf, cf=cf, final=final),
        grid=(T // tm,),
        in_specs=in_specs,
        out_specs=pl.BlockSpec((tm, D), lambda i: (i, 0)),
        out_shape=jax.ShapeDtypeStruct((T, D), f32),
        scratch_shapes=[pltpu.VMEM((tm, F), bf16)],
        compiler_params=pltpu.CompilerParams(dimension_semantics=("parallel",), vmem_limit_bytes=VMEM_LIMIT),
        name="ffn_final" if final else "ffn",
    )(*args)


def _lru_kernel(x_ref, g_ref, w_ref, cw_ref, cb_ref, wax_ref, ba_ref, bx_ref, lam_ref, conv0_ref, h0_ref,
                y_ref, convo_ref, ho_ref, tail_ref, hc_ref, a_scr, u_scr, *, width):
    W = width
    nb, rows = x_ref.shape[0], x_ref.shape[1]

    @pl.when(pl.program_id(1) == 0)
    def _():
        tail_ref[...] = conv0_ref[...]
        hc_ref[...] = h0_ref[...]

    sub = lax.broadcasted_iota(jnp.int32, (rows, W), 0) & (SUBLANES - 1)
    sp = _softplus(-lam_ref[...])
    ng = W // LRU_GROUP
    gates = []
    for bi in range(nb):
        n = _rms(x_ref[bi], g_ref[...]).astype(bf16)
        p = _dot(n, w_ref[...])
        xl = p[:, :W]
        gates.append(p[:, W:])
        xa = _causal_conv(tail_ref[bi], xl, cw_ref[...], cb_ref[...])
        new_tail = xl[rows - SUBLANES:, :]
        tail_ref[bi] = new_tail
        convo_ref[bi] = new_tail

        xab = xa.astype(bf16)
        parts = [_dot(xab[:, k * LRU_GROUP:(k + 1) * LRU_GROUP], wax_ref[k]) for k in range(ng)]
        r = jax.nn.sigmoid(jnp.concatenate([q[:, :LRU_GROUP] for q in parts], axis=1) + ba_ref[...])
        i = jax.nn.sigmoid(jnp.concatenate([q[:, LRU_GROUP:] for q in parts], axis=1) + bx_ref[...])
        log_a = (-LRU_C) * r * sp
        a = jnp.exp(log_a)
        th = jnp.tanh(log_a)
        u = jnp.sqrt(-2.0 * th / (1.0 - th)) * (i * xa)

        for s in (1, 2, 4):
            keep = sub >= s
            u = jnp.where(keep, a * pltpu.roll(u, s, 0) + u, u)
            a = jnp.where(keep, a * pltpu.roll(a, s, 0), a)
        a_scr[bi] = a
        u_scr[bi] = u

    def group(k, hs):
        off = pl.multiple_of(k * SUBLANES, SUBLANES)
        out = []
        for bi in range(nb):
            hg = a_scr[bi, pl.ds(off, SUBLANES), :] * hs[bi] + u_scr[bi, pl.ds(off, SUBLANES), :]
            u_scr[bi, pl.ds(off, SUBLANES), :] = hg
            out.append(hg[SUBLANES - 1:SUBLANES, :])
        return tuple(out)

    h_last = lax.fori_loop(0, rows // SUBLANES, group, tuple(hc_ref[bi] for bi in range(nb)))
    for bi in range(nb):
        hc_ref[bi] = h_last[bi]
        ho_ref[bi] = h_last[bi]
        y_ref[bi] = u_scr[bi] * _gelu_tanh(gates[bi])


def _lru(x, wts, l, conv8, h0, tl):
    B, L, D = x.shape
    W = conv8.shape[-1]
    nb = SEQ_PER_STEP
    lmap = lambda b, t: (l, 0, 0)
    return pl.pallas_call(
        functools.partial(_lru_kernel, width=W),
        grid=(B // nb, L // tl),
        in_specs=[
            pl.BlockSpec((nb, tl, D), lambda b, t: (b, t, 0)),
            _resident((None, 1, D), lmap),
            _resident((None, D, 2 * W), lmap),
            _resident((None, CONV_W, W), lmap),
            _resident((None, 1, W), lmap),
            _resident((None, W // LRU_GROUP, LRU_GROUP, 2 * LRU_GROUP), lambda b, t: (l, 0, 0, 0)),
            _resident((None, 1, W), lmap),
            _resident((None, 1, W), lmap),
            _resident((None, 1, W), lmap),
            pl.BlockSpec((nb, SUBLANES, W), lambda b, t: (b, 0, 0)),
            pl.BlockSpec((nb, 1, W), lambda b, t: (b, 0, 0)),
        ],
        out_specs=[
            pl.BlockSpec((nb, tl, W), lambda b, t: (b, t, 0)),
            pl.BlockSpec((nb, SUBLANES, W), lambda b, t: (b, 0, 0)),
            pl.BlockSpec((nb, 1, W), lambda b, t: (b, 0, 0)),
        ],
        out_shape=[
            jax.ShapeDtypeStruct((B, L, W), f32),
            jax.ShapeDtypeStruct((B, SUBLANES, W), f32),
            jax.ShapeDtypeStruct((B, 1, W), f32),
        ],
        scratch_shapes=[
            pltpu.VMEM((nb, SUBLANES, W), f32),
            pltpu.VMEM((nb, 1, W), f32),
            pltpu.VMEM((nb, tl, W), f32),
            pltpu.VMEM((nb, tl, W), f32),
        ],
        compiler_params=pltpu.CompilerParams(dimension_semantics=("arbitrary", "arbitrary"),
                                             vmem_limit_bytes=VMEM_LIMIT),
        name="lru",
    )(x, wts["mix_norm"], wts["w_lru"], wts["lru_conv_w"], wts["lru_conv_b"], wts["lru_wax"], wts["lru_b_a"],
      wts["lru_b_x"], wts["lru_lambda"], conv8, h0)


def _split3(x):
    a = x.astype(bf16)
    r = x - a.astype(f32)
    b = r.astype(bf16)
    return a, b, (r - b.astype(f32)).astype(bf16)


def _expand_heads(x, e3_ref):
    return _dot(jnp.concatenate(_split3(x), axis=1), e3_ref[...])


def _ssd_kernel(x_ref, g_ref, wzx_ref, wdt_ref, wdte_ref, wdto_ref, cw_ref, cb_ref, dtb_ref, dtbe_ref, dtbo_ref,
                alog_ref, aloge_ref, alogo_ref, dskip_ref, ng_ref, e3_ref, conv0_ref, h0_ref,
                y_ref, convo_ref, ho_ref,
                tail_ref, hst_ref, y_scr, e_scr, st_scr, *, q):
    @pl.when(pl.program_id(1) == 0)
    def _():
        tail_ref[...] = conv0_ref[...]
        hst_ref[...] = h0_ref[...]

    for bi in range(x_ref.shape[0]):
        _ssd_tile(x_ref.at[bi], g_ref, wzx_ref, wdt_ref, wdte_ref, wdto_ref, cw_ref, cb_ref, dtb_ref, dtbe_ref,
                  dtbo_ref, alog_ref, aloge_ref, alogo_ref, dskip_ref, ng_ref, e3_ref,
                  y_ref.at[bi], convo_ref.at[bi], ho_ref.at[bi],
                  tail_ref.at[bi], hst_ref.at[bi], y_scr.at[bi], e_scr.at[bi], st_scr.at[bi], q=q)


def _ssd_tile(x_ref, g_ref, wzx_ref, wdt_ref, wdte_ref, wdto_ref, cw_ref, cb_ref, dtb_ref, dtbe_ref, dtbo_ref,
              alog_ref, aloge_ref, alogo_ref, dskip_ref, ng_ref, e3_ref,
              y_ref, convo_ref, ho_ref, tail_ref, hst_ref, y_scr, e_scr, st_scr, *, q):
    rows = x_ref.shape[0]
    nch = rows // q
    inner = SSD_HEADS * SSD_HEAD_DIM
    gn = SSD_GROUPS * SSD_STATE
    ppg = SSD_HEADS // 2 // SSD_GROUPS
    gw = inner // SSD_GROUPS

    n = _rms(x_ref[...], g_ref[...]).astype(bf16)
    p = _dot(n, wzx_ref[...])
    z = p[:, :inner]
    xr = p[:, inner:]
    xc = _silu(_causal_conv(tail_ref[...], xr, cw_ref[...], cb_ref[...]))
    new_tail = xr[rows - SUBLANES:, :]
    tail_ref[...] = new_tail
    convo_ref[...] = new_tail
    xs = xc[:, 0:inner]
    bmb = xc[:, inner:inner + gn].astype(bf16)
    cmb = xc[:, inner + gn:inner + 2 * gn].astype(bf16)

    ri = lax.broadcasted_iota(jnp.int32, (rows, rows), 0)
    ci = lax.broadcasted_iota(jnp.int32, (rows, rows), 1)
    same_blk = (ri // q) == (ci // q)
    tri_bd = (same_blk & (ci <= ri)).astype(f32)
    tri_bd_t = (same_blk & (ri <= ci)).astype(f32)
    li = lax.broadcasted_iota(jnp.int32, (q, 2 * q), 0)
    lj = lax.broadcasted_iota(jnp.int32, (q, 2 * q), 1)
    causal2 = jnp.where(lj >= q, lj - q, lj) <= li
    lane_hi = lax.broadcasted_iota(jnp.int32, (q, LANES), 1) >= SSD_HEAD_DIM
    a_neg = -jnp.exp(alog_ref[...])
    a_neg_e = -jnp.exp(aloge_ref[...])
    a_neg_o = -jnp.exp(alogo_ref[...])

    dt = _softplus(_dot(n, wdt_ref[...]) + dtb_ref[...])
    cum = jnp.dot(tri_bd, dt * a_neg, precision=_HI, preferred_element_type=f32)
    last = jnp.concatenate([jnp.broadcast_to(cum[(c + 1) * q - 1:(c + 1) * q, :], (q, LANES)) for c in range(nch)],
                           axis=0)
    decay_end = jnp.exp(last - cum) * dt
    cum_x = _expand_heads(cum, e3_ref)
    y_scr[...] = dskip_ref[...] * xs
    e_scr[...] = jnp.exp(cum_x)
    xd = (xs * _expand_heads(decay_end, e3_ref)).astype(bf16)
    dt_e = _softplus(_dot_nt(wdte_ref[...], n) + dtbe_ref[...])
    dt_o = _softplus(_dot_nt(wdto_ref[...], n) + dtbo_ref[...])
    cum_e = jnp.dot(dt_e * a_neg_e, tri_bd_t, precision=_HI, preferred_element_type=f32)
    cum_o = jnp.dot(dt_o * a_neg_o, tri_bd_t, precision=_HI, preferred_element_type=f32)

    for c in range(nch):
        rc = slice(c * q, (c + 1) * q)
        dt_t2 = jnp.concatenate([dt_e[:, rc], dt_o[:, rc]], axis=1)
        cum_t2 = jnp.concatenate([cum_e[:, rc], cum_o[:, rc]], axis=1)
        for g in range(SSD_GROUPS):
            bgb = bmb[rc, g * SSD_STATE:(g + 1) * SSD_STATE]
            cgb = cmb[rc, g * SSD_STATE:(g + 1) * SSD_STATE]
            cb2 = _dot_nt(cgb, jnp.concatenate([bgb, bgb], axis=0))
            for pr in range(ppg):
                m = g * ppg + pr
                ps = slice(m * LANES, (m + 1) * LANES)
                cx = cum_x[rc, ps]
                if 2 * q != LANES:
                    cx = jnp.concatenate([cx[:, :q], cx[:, SSD_HEAD_DIM:SSD_HEAD_DIM + q]], axis=1)
                wm = cb2 * jnp.where(causal2, jnp.exp(cx - cum_t2[m:m + 1, :]), 0.0) * dt_t2[m:m + 1, :]
                xp = xs[rc, ps]
                x_bd = jnp.concatenate([jnp.where(lane_hi, 0.0, xp), jnp.where(lane_hi, xp, 0.0)], axis=0)
                y_scr[rc, ps] += _dot(wm.astype(bf16), x_bd.astype(bf16))
            gs = slice(g * gw, (g + 1) * gw)
            st_scr[c, :, gs] = _dot_tn(bgb, xd[rc, gs])

    for c in range(nch):
        rc = slice(c * q, (c + 1) * q)
        for g in range(SSD_GROUPS):
            gs = slice(g * gw, (g + 1) * gw)
            y_off = _dot(cmb[rc, g * SSD_STATE:(g + 1) * SSD_STATE], hst_ref[:, gs].astype(bf16))
            y_scr[rc, gs] += y_off * e_scr[rc, gs]
            hst_ref[:, gs] = e_scr[(c + 1) * q - 1:(c + 1) * q, gs] * hst_ref[:, gs] + st_scr[c, :, gs]
    ho_ref[...] = hst_ref[...]

    y = y_scr[...] * _silu(z)
    ng = ng_ref[...]
    for g in range(SSD_GROUPS):
        sl = slice(g * gw, (g + 1) * gw)
        y_ref[:, sl] = _rms(y[:, sl], ng[:, sl])


def _ssd(x, wts, l, conv8, h0t, tq, q):
    B, L, D = x.shape
    inner = SSD_HEADS * SSD_HEAD_DIM
    cdim = inner + 2 * SSD_GROUPS * SSD_STATE
    half = SSD_HEADS // 2
    nb = SEQ_PER_STEP
    lmap = lambda b, t: (l, 0, 0)
    return pl.pallas_call(
        functools.partial(_ssd_kernel, q=q),
        grid=(B // nb, L // tq),
        in_specs=[
            pl.BlockSpec((nb, tq, D), lambda b, t: (b, t, 0)),
            _resident((None, 1, D), lmap),
            _resident((None, D, inner + cdim), lmap),
            _resident((None, D, LANES), lmap),
            _resident((None, half, D), lmap),
            _resident((None, half, D), lmap),
            _resident((None, CONV_W, cdim), lmap),
            _resident((None, 1, cdim), lmap),
            _resident((None, 1, LANES), lmap),
            _resident((None, half, 1), lmap),
            _resident((None, half, 1), lmap),
            _resident((None, 1, LANES), lmap),
            _resident((None, half, 1), lmap),
            _resident((None, half, 1), lmap),
            _resident((None, 1, inner), lmap),
            _resident((None, 1, inner), lmap),
            _resident((3 * LANES, inner), lambda b, t: (0, 0)),
            pl.BlockSpec((nb, SUBLANES, cdim), lambda b, t: (b, 0, 0)),
            pl.BlockSpec((nb, SSD_STATE, inner), lambda b, t: (b, 0, 0)),
        ],
        out_specs=[
            pl.BlockSpec((nb, tq, inner), lambda b, t: (b, t, 0)),
            pl.BlockSpec((nb, SUBLANES, cdim), lambda b, t: (b, 0, 0)),
            pl.BlockSpec((nb, SSD_STATE, inner), lambda b, t: (b, 0, 0)),
        ],
        out_shape=[
            jax.ShapeDtypeStruct((B, L, inner), f32),
            jax.ShapeDtypeStruct((B, SUBLANES, cdim), f32),
            jax.ShapeDtypeStruct((B, SSD_STATE, inner), f32),
        ],
        scratch_shapes=[
            pltpu.VMEM((nb, SUBLANES, cdim), f32),
            pltpu.VMEM((nb, SSD_STATE, inner), f32),
            pltpu.VMEM((nb, tq, inner), f32),
            pltpu.VMEM((nb, tq, inner), f32),
            pltpu.VMEM((nb, tq // q, SSD_STATE, inner), f32),
        ],
        compiler_params=pltpu.CompilerParams(dimension_semantics=("arbitrary", "arbitrary"),
                                             vmem_limit_bytes=VMEM_LIMIT),
        name="ssd",
    )(x, wts["mix_norm"], wts["w_zx"], wts["w_dt"], wts["w_dt_e"], wts["w_dt_o"], wts["ssd_conv_w"],
      wts["ssd_conv_b"], wts["ssd_dt_bias"], wts["ssd_dt_bias_e"], wts["ssd_dt_bias_o"], wts["ssd_a_log"],
      wts["ssd_a_log_e"], wts["ssd_a_log_o"], wts["ssd_d"], wts["ssd_norm"], wts["ssd_expand3"], conv8, h0t)


def _attn_kernel(slope_ref, sink_ref, x_ref, g_ref, w_ref, pk_ref, pv_ref,
                 y_ref, ko_ref, vo_ref, kp_scr, vp_scr, *, pos0, sub):
    @pl.when(pl.program_id(1) == 0)
    def _():
        kp_scr[...] = pk_ref[...]
        vp_scr[...] = pv_ref[...]

    for bi in range(x_ref.shape[0]):
        _attn_tile(slope_ref, sink_ref, x_ref.at[bi], g_ref, w_ref, y_ref.at[bi], ko_ref.at[bi], vo_ref.at[bi],
                   kp_scr.at[bi], vp_scr.at[bi], pos0=pos0, sub=sub)


def _attn_tile(slope_ref, sink_ref, x_ref, g_ref, w_ref, y_ref, ko_ref, vo_ref, kp_scr, vp_scr, *, pos0, sub):
    rows = x_ref.shape[0]
    s_len = WINDOW + sub
    aw = N_HEADS * HEAD_DIM
    kw = N_KV_HEADS * HEAD_DIM
    gqa = N_HEADS // N_KV_HEADS
    t = pl.program_id(1)

    n = _rms(x_ref[...], g_ref[...]).astype(bf16)
    p = _dot(n, w_ref[...])
    qq = p[:, :aw] * (HEAD_DIM ** -0.5)
    kk = jnp.concatenate([kp_scr[...], p[:, aw:aw + kw]], axis=0)
    vv = jnp.concatenate([vp_scr[...], p[:, aw + kw:aw + 2 * kw]], axis=0)
    k_new = kk[rows:, :]
    v_new = vv[rows:, :]
    kp_scr[...] = k_new
    vp_scr[...] = v_new
    ko_ref[...] = k_new
    vo_ref[...] = v_new

    iq = lax.broadcasted_iota(jnp.int32, (sub, s_len), 0)
    ik = lax.broadcasted_iota(jnp.int32, (sub, s_len), 1)
    neg_dist = -jnp.abs(iq - (ik - WINDOW)).astype(f32)
    qc = iq // CHUNK
    kc = ik // CHUNK - WIN_CHUNKS
    in_band = (kc >= qc - WIN_CHUNKS) & (kc <= qc)
    lane_q = lax.broadcasted_iota(jnp.int32, (sub, LANES), 1) >= HEAD_DIM
    lane_k = lax.broadcasted_iota(jnp.int32, (WINDOW + rows, LANES), 1) >= HEAD_DIM

    k_both, v_lo, v_hi = [], [], []
    for kvh in range(N_KV_HEADS):
        tile = slice((kvh // 2) * LANES, (kvh // 2 + 1) * LANES)
        odd = kvh % 2 == 1
        k_own = jnp.where(lane_k, kk[:, tile], 0.0) if odd else jnp.where(lane_k, 0.0, kk[:, tile])
        v_own = jnp.where(lane_k, vv[:, tile], 0.0) if odd else jnp.where(lane_k, 0.0, vv[:, tile])
        v_other = pltpu.roll(v_own, HEAD_DIM, 1)
        k_both.append((k_own + pltpu.roll(k_own, HEAD_DIM, 1)).astype(bf16))
        v_lo.append((v_other if odd else v_own).astype(bf16))
        v_hi.append((v_own if odd else v_other).astype(bf16))

    for st in range(rows // sub):
        r0 = st * sub
        k_pos = ik + (pos0 - WINDOW + r0 + t * rows)
        bias_base = jnp.where(in_band & (k_pos >= 0), neg_dist, -MASK_DIST)
        for m in range(N_HEADS // 2):
            kvh = (2 * m) // gqa
            qp = qq[r0:r0 + sub, m * LANES:(m + 1) * LANES]
            o_pair = jnp.zeros((sub, LANES), f32)
            for half in range(2):
                h = 2 * m + half
                qh = (jnp.where(lane_q, qp, 0.0) if half else jnp.where(lane_q, 0.0, qp)).astype(bf16)
                s = _dot_nt(qh, k_both[kvh][r0:r0 + s_len]) + slope_ref[h] * bias_base
                mx = jnp.maximum(jnp.max(s, axis=-1, keepdims=True), sink_ref[h])
                e = jnp.exp(s - mx)
                den = jnp.sum(e, axis=-1, keepdims=True) + jnp.exp(sink_ref[h] - mx)
                vh = (v_hi if half else v_lo)[kvh][r0:r0 + s_len]
                o_pair = o_pair + _dot(e.astype(bf16), vh) * (1.0 / den)
            y_ref[r0:r0 + sub, m * LANES:(m + 1) * LANES] = o_pair


def _attn(x, wts, l, slopes, sink, prev_k, prev_v, tq, pos0):
    B, L, D = x.shape
    aw = N_HEADS * HEAD_DIM
    kw = N_KV_HEADS * HEAD_DIM
    nb = SEQ_PER_STEP
    lmap = lambda b, t: (l, 0, 0)
    smem = pl.BlockSpec(memory_space=pltpu.SMEM)
    return pl.pallas_call(
        functools.partial(_attn_kernel, pos0=pos0, sub=min(tq, ATT_SUB)),
        grid=(B // nb, L // tq),
        in_specs=[
            smem, smem,
            pl.BlockSpec((nb, tq, D), lambda b, t: (b, t, 0)),
            _resident((None, 1, D), lmap),
            _resident((None, D, aw + 2 * kw), lmap),
            pl.BlockSpec((nb, WINDOW, kw), lambda b, t: (b, 0, 0)),
            pl.BlockSpec((nb, WINDOW, kw), lambda b, t: (b, 0, 0)),
        ],
        out_specs=[
            pl.BlockSpec((nb, tq, aw), lambda b, t: (b, t, 0)),
            pl.BlockSpec((nb, WINDOW, kw), lambda b, t: (b, 0, 0)),
            pl.BlockSpec((nb, WINDOW, kw), lambda b, t: (b, 0, 0)),
        ],
        out_shape=[
            jax.ShapeDtypeStruct((B, L, aw), f32),
            jax.ShapeDtypeStruct((B, WINDOW, kw), f32),
            jax.ShapeDtypeStruct((B, WINDOW, kw), f32),
        ],
        scratch_shapes=[pltpu.VMEM((nb, WINDOW, kw), f32), pltpu.VMEM((nb, WINDOW, kw), f32)],
        compiler_params=pltpu.CompilerParams(dimension_semantics=("arbitrary", "arbitrary"),
                                             vmem_limit_bytes=VMEM_LIMIT),
        name="attn",
    )(slopes, sink, x, wts["mix_norm"], wts["w_qkv"], prev_k, prev_v)


def _merge_kernel(x_ref, ya_ref, yb_ref, yc_ref, g_ref, wg_ref, wb_ref, wo_ref, o_ref):
    x = x_ref[...]
    D = x.shape[-1]
    n = _rms(x, g_ref[...]).astype(bf16)
    merged = jnp.zeros(x.shape, f32)
    for b, y_ref in enumerate((ya_ref, yb_ref, yc_ref)):
        bw = y_ref.shape[-1]
        gate = jax.nn.sigmoid(_dot(n, wg_ref[:, b * D:(b + 1) * D]))
        merged = merged + gate * _dot(y_ref[...].astype(bf16), wb_ref[b * bw:(b + 1) * bw, :])
    o_ref[...] = x + _dot(merged.astype(bf16), wo_ref[...])


def _merge(x, ya, yb, yc, wts, l):
    T, D = x.shape
    tm = min(T, 512)
    lmap = lambda i: (l, 0, 0)
    tok = lambda w: pl.BlockSpec((tm, w), lambda i: (i, 0))
    return pl.pallas_call(
        _merge_kernel,
        grid=(T // tm,),
        in_specs=[
            tok(D), tok(ya.shape[-1]), tok(yb.shape[-1]), tok(yc.shape[-1]),
            _resident((None, 1, D), lmap),
            _resident((None, D, 3 * D), lmap),
            _resident((None, wts["w_branch"].shape[1], D), lmap),
            _resident((None, D, D), lmap),
        ],
        out_specs=tok(D),
        out_shape=jax.ShapeDtypeStruct((T, D), f32),
        compiler_params=pltpu.CompilerParams(dimension_semantics=("parallel",), vmem_limit_bytes=VMEM_LIMIT),
        name="merge",
    )(x, ya, yb, yc, wts["mix_norm"], wts["w_gates"], wts["w_branch"], wts["w_out"])


def _block_diag_groups(w):
    nl, nb, bs, _ = w.shape
    per = LRU_GROUP // bs
    w = w.reshape(nl, nb // per, per, bs, bs)
    eye = jnp.eye(per, dtype=w.dtype)
    out = w[:, :, :, :, None, :] * eye[None, None, :, None, :, None]
    return out.reshape(nl, nb // per, LRU_GROUP, LRU_GROUP)


def _prep_weights(ffn_norm, ffn_w_gate, ffn_w_up, ffn_w_down, mix_norm, w_in, lru_conv_w, lru_conv_b, lru_w_a,
                  lru_b_a, lru_w_x, lru_b_x, lru_lambda, ssd_conv_w, ssd_conv_b, ssd_dt_bias, ssd_a_log, ssd_d,
                  ssd_norm, w_branch, w_out):
    nl, D = mix_norm.shape
    W = lru_conv_w.shape[-1]
    inner = SSD_HEADS * SSD_HEAD_DIM
    cdim = ssd_conv_w.shape[-1]
    aw = N_HEADS * HEAD_DIM
    kw = N_KV_HEADS * HEAD_DIM
    o = [0]
    for s in (W, W, inner, cdim, SSD_HEADS, aw, kw, kw, 3 * D):
        o.append(o[-1] + s)
    row = lambda a: a.reshape(nl, 1, -1)
    pad_lanes = lambda a: jnp.pad(a, [(0, 0)] * (a.ndim - 1) + [(0, LANES - a.shape[-1])])
    w_dt = w_in[:, :, o[4]:o[5]]
    w_dt_t = jnp.swapaxes(w_dt, 1, 2).astype(bf16)
    col = lambda a: a[:, :, None]
    return {
        "ffn_norm": ffn_norm.reshape(nl, 2, 1, D),
        "ffn_w_gate": ffn_w_gate.astype(bf16), "ffn_w_up": ffn_w_up.astype(bf16), "ffn_w_down": ffn_w_down.astype(bf16),
        "mix_norm": row(mix_norm),
        "w_lru": w_in[:, :, o[0]:o[2]].astype(bf16),
        "w_zx": w_in[:, :, o[2]:o[4]].astype(bf16),
        "w_dt": pad_lanes(w_dt).astype(bf16),
        "w_dt_e": w_dt_t[:, 0::2], "w_dt_o": w_dt_t[:, 1::2],
        "w_qkv": w_in[:, :, o[5]:o[8]].astype(bf16),
        "w_gates": w_in[:, :, o[8]:o[9]].astype(bf16),
        "lru_conv_w": lru_conv_w, "lru_conv_b": row(lru_conv_b),
        "lru_wax": jnp.concatenate([_block_diag_groups(lru_w_a), _block_diag_groups(lru_w_x)], axis=-1).astype(bf16),
        "lru_b_a": row(lru_b_a), "lru_b_x": row(lru_b_x), "lru_lambda": row(lru_lambda),
        "ssd_conv_w": ssd_conv_w, "ssd_conv_b": row(ssd_conv_b),
        "ssd_dt_bias": pad_lanes(row(ssd_dt_bias)),
        "ssd_dt_bias_e": col(ssd_dt_bias[:, 0::2]), "ssd_dt_bias_o": col(ssd_dt_bias[:, 1::2]),
        "ssd_a_log": pad_lanes(row(ssd_a_log)),
        "ssd_a_log_e": col(ssd_a_log[:, 0::2]), "ssd_a_log_o": col(ssd_a_log[:, 1::2]),
        "ssd_d": row(jnp.repeat(ssd_d, SSD_HEAD_DIM, axis=-1)),
        "ssd_norm": row(ssd_norm),
        "ssd_expand3": (jnp.arange(3 * LANES)[:, None] % LANES
                        == (jnp.arange(inner) // SSD_HEAD_DIM)[None, :]).astype(bf16),
        "w_branch": w_branch.astype(bf16), "w_out": w_out.astype(bf16),
    }


def _pad_conv_state(s):
    return jnp.pad(s, ((0, 0), (SUBLANES - s.shape[1], 0), (0, 0)))


def _layer(x, wts, l, slopes, sink, state, tiles, pos0, final_norm):
    B, L, D = x.shape
    cache_k, cache_v, lru_conv, lru_h, ssd_conv, ssd_h = state
    tl, tq_ssd, q, tq_att = tiles
    kw = N_KV_HEADS * HEAD_DIM
    inner = SSD_HEADS * SSD_HEAD_DIM
    ffn_w = (wts["ffn_norm"], wts["ffn_w_gate"], wts["ffn_w_up"], wts["ffn_w_down"])

    x = _ffn(x.reshape(B * L, D), *ffn_w, l, 0).reshape(B, L, D)
    ya, lru_conv8, lru_hn = _lru(x, wts, l, _pad_conv_state(lru_conv), lru_h[:, None, :], tl)
    ssd_ht = jnp.swapaxes(ssd_h.reshape(B, inner, SSD_STATE), 1, 2)
    yb, ssd_conv8, ssd_htn = _ssd(x, wts, l, _pad_conv_state(ssd_conv), ssd_ht, tq_ssd, q)
    ssd_hn = jnp.swapaxes(ssd_htn, 1, 2)
    yc, kn, vn = _attn(x, wts, l, slopes, sink, cache_k.reshape(B, WINDOW, kw), cache_v.reshape(B, WINDOW, kw),
                       tq_att, pos0)
    flat = lambda a: a.reshape(B * L, a.shape[-1])
    x = _merge(flat(x), flat(ya), flat(yb), flat(yc), wts, l)
    x = _ffn(x, *ffn_w, l, 1, final_norm=final_norm).reshape(B, L, D)
    new_state = (kn.reshape(B, WINDOW, N_KV_HEADS, HEAD_DIM), vn.reshape(B, WINDOW, N_KV_HEADS, HEAD_DIM),
                 lru_conv8[:, SUBLANES - (CONV_W - 1):], lru_hn[:, 0],
                 ssd_conv8[:, SUBLANES - (CONV_W - 1):], ssd_hn.reshape(B, SSD_HEADS, SSD_HEAD_DIM, SSD_STATE))
    return x, new_state


def kernel(x_prompt, x_sample, cache_attn_k, cache_attn_v, state_lru_conv, state_lru_h, state_ssd_conv, state_ssd_h, ffn_norm, ffn_w_gate, ffn_w_up, ffn_w_down, mix_norm, w_in, lru_conv_w, lru_conv_b, lru_w_a, lru_b_a, lru_w_x, lru_b_x, lru_lambda, ssd_conv_w, ssd_conv_b, ssd_dt_bias, ssd_a_log, ssd_d, ssd_norm, attn_sink, w_branch, w_out, final_norm):
    depth = mix_norm.shape[0]
    bp, lp, D = x_prompt.shape
    bs, ls, _ = x_sample.shape
    wts = _prep_weights(ffn_norm, ffn_w_gate, ffn_w_up, ffn_w_down, mix_norm, w_in, lru_conv_w, lru_conv_b, lru_w_a,
                        lru_b_a, lru_w_x, lru_b_x, lru_lambda, ssd_conv_w, ssd_conv_b, ssd_dt_bias, ssd_a_log, ssd_d,
                        ssd_norm, w_branch, w_out)
    slopes = jnp.exp2(-8.0 * (jnp.arange(N_HEADS, dtype=f32) + 1.0) / N_HEADS)
    fin = final_norm.reshape(1, D)
    kw = N_KV_HEADS * HEAD_DIM
    zero_state = (jnp.zeros((bp, WINDOW, kw), f32), jnp.zeros((bp, WINDOW, kw), f32),
                  jnp.zeros((bp, CONV_W - 1, lru_conv_w.shape[-1]), f32), jnp.zeros((bp, lru_conv_w.shape[-1]), f32),
                  jnp.zeros((bp, CONV_W - 1, ssd_conv_w.shape[-1]), f32),
                  jnp.zeros((bp, SSD_HEADS, SSD_HEAD_DIM, SSD_STATE), f32))
    tiles_p = (min(lp, 256), min(lp, 256), CHUNK, min(lp, 256))
    tiles_s = (ls, ls, ls, ls)
    xp, xs = x_prompt, x_sample
    p_new = [[] for _ in range(6)]
    s_new = [[] for _ in range(6)]
    for l in range(depth):
        last = fin if l == depth - 1 else None
        xp, st_p = _layer(xp, wts, l, slopes, attn_sink[l], zero_state, tiles_p, 0, last)
        st_s = (cache_attn_k[l], cache_attn_v[l], state_lru_conv[l], state_lru_h[l], state_ssd_conv[l], state_ssd_h[l])
        xs, st_sn = _layer(xs, wts, l, slopes, attn_sink[l], st_s, tiles_s, PAST_LEN, last)
        for j in range(6):
            p_new[j].append(st_p[j])
            s_new[j].append(st_sn[j])
    pk, pv, plc, plh, psc, psh = [jnp.stack(t, axis=0) for t in p_new]
    sk, sv, slc, slh, ssc, ssh = [jnp.stack(t, axis=0) for t in s_new]
    return (xp, xs, pk, pv, plc, plh, psc, psh, sk, sv, slc, slh, ssc, ssh)
```

```python
import functools
import math

import jax
import jax.numpy as jnp
from jax import lax
from jax.experimental import pallas as pl
from jax.experimental.pallas import tpu as pltpu

f32 = jnp.float32
bf16 = jnp.bfloat16

EPS = 1e-6
CHUNK = 64
CONV_W = 4
LRU_BLOCKS = 16
LRU_C = 8.0
LRU_GROUP = 256
SSD_HEADS = 16
SSD_HEAD_DIM = 64
SSD_GROUPS = 2
SSD_STATE = 128
N_HEADS = 16
N_KV_HEADS = 4
HEAD_DIM = 64
WINDOW = 128
WIN_CHUNKS = WINDOW // CHUNK
PAST_LEN = 4096
NEG = -1e30
MASK_DIST = -NEG * 2.0 ** 8
ATT_SUB = 128
ATT_STACK_ROWS = 64
SEQ_PER_STEP = 1
LANES = 128
SUBLANES = 8
VMEM_LIMIT = 56 * 1024 * 1024

_HI = lax.Precision.HIGHEST


def _rms(x, g):
    return x * lax.rsqrt(jnp.mean(x * x, axis=-1, keepdims=True) + EPS) * g


def _silu(x):
    return x * jax.nn.sigmoid(x)


def _softplus(x):
    return jnp.maximum(x, 0.0) + jnp.log1p(jnp.exp(-jnp.abs(x)))


def _gelu_tanh(x):
    c = math.sqrt(2.0 / math.pi)
    return x * (0.5 * (1.0 + jnp.tanh(c * (x + 0.044715 * (x * x * x)))))


def _dot(a, b):
    return jnp.dot(a, b, preferred_element_type=f32)


def _dot_nt(a, b):
    return lax.dot_general(a, b, (((1,), (1,)), ((), ())), preferred_element_type=f32)


def _dot_tn(a, b):
    return lax.dot_general(a, b, (((0,), (0,)), ((), ())), preferred_element_type=f32)


def _causal_conv(tail_ref, x, cw, cb):
    rows = x.shape[0]
    ext = jnp.concatenate([tail_ref[...], x], axis=0)
    y = cb + cw[CONV_W - 1:CONV_W, :] * x
    for s in range(1, CONV_W):
        y = y + cw[CONV_W - 1 - s:CONV_W - s, :] * pltpu.roll(ext, s, 0)[SUBLANES:SUBLANES + rows]
    tail_ref[...] = x[rows - SUBLANES:, :]
    return y


def _resident(shape, index_map):
    return pl.BlockSpec(shape, index_map, pipeline_mode=pl.Buffered(1))


def _ffn_kernel(*refs, n_chunks, cf, final):
    if final:
        x_ref, g_ref, wg_ref, wu_ref, wd_ref, fg_ref, o_ref, a_ref = refs
    else:
        x_ref, g_ref, wg_ref, wu_ref, wd_ref, o_ref, a_ref = refs
    x = x_ref[...]
    n = _rms(x, g_ref[...]).astype(bf16)
    for c in range(n_chunks):
        sl = slice(c * cf, (c + 1) * cf)
        gate = _dot(n, wg_ref[:, sl])
        up = _dot(n, wu_ref[:, sl])
        a_ref[:, sl] = (_silu(gate) * up).astype(bf16)
    y = x + 0.5 * _dot(a_ref[...], wd_ref[...])
    if final:
        y = _rms(y, fg_ref[...])
    o_ref[...] = y


def _ffn(x, norm, wg, wu, wd, l, j, final_norm=None):
    T, D = x.shape
    F = wg.shape[-1]
    tm = min(T, 512)
    cf = 256
    final = final_norm is not None
    wmap = lambda i: (l, j, 0, 0)
    in_specs = [
        pl.BlockSpec((tm, D), lambda i: (i, 0)),
        _resident((None, None, 1, D), wmap),
        _resident((None, None, D, F), wmap),
        _resident((None, None, D, F), wmap),
        _resident((None, None, F, D), wmap),
    ]
    args = [x, norm, wg, wu, wd]
    if final:
        in_specs.append(_resident((1, D), lambda i: (0, 0)))
        args.append(final_norm)
    return pl.pallas_call(
        functools.partial(_ffn_kernel, n_chunks=F // cf, cf=cf, final=final),
        grid=(T // tm,),
        in_specs=in_specs,
        out_specs=pl.BlockSpec((tm, D), lambda i: (i, 0)),
        out_shape=jax.ShapeDtypeStruct((T, D), f32),
        scratch_shapes=[pltpu.VMEM((tm, F), bf16)],
        compiler_params=pltpu.CompilerParams(dimension_semantics=("parallel",), vmem_limit_bytes=VMEM_LIMIT),
        name="ffn_final" if final else "ffn",
    )(*args)


def _lru_kernel(x_ref, g_ref, w_ref, cw_ref, cb_ref, wax_ref, ba_ref, bx_ref, lam_ref, conv0_ref, h0_ref,
                y_ref, convo_ref, ho_ref, tail_ref, hc_ref, a_scr, u_scr, gg_scr, *, width):
    W = width
    nb, rows = x_ref.shape[0], x_ref.shape[1]

    @pl.when(pl.program_id(1) == 0)
    def _():
        tail_ref[...] = conv0_ref[...]
        hc_ref[...] = h0_ref[...]

    G = LRU_GROUP
    sub = lax.broadcasted_iota(jnp.int32, (rows, G), 0) & (SUBLANES - 1)
    sp = _softplus(-lam_ref[...])
    for bi in range(nb):
        n = _rms(x_ref[bi], g_ref[...]).astype(bf16)
        for k in range(W // G):
            cs = slice(k * G, (k + 1) * G)
            xl = _dot(n, w_ref[:, cs])
            gg_scr[bi, :, cs] = _gelu_tanh(_dot(n, w_ref[:, W + k * G:W + (k + 1) * G]))
            xa = _causal_conv(tail_ref.at[bi, :, cs], xl, cw_ref[:, cs], cb_ref[:, cs])
            convo_ref[bi, :, cs] = xl[rows - SUBLANES:, :]

            pa = _dot(xa.astype(bf16), wax_ref[k])
            r = jax.nn.sigmoid(pa[:, :G] + ba_ref[:, cs])
            i = jax.nn.sigmoid(pa[:, G:] + bx_ref[:, cs])
            log_a = (-LRU_C) * r * sp[:, cs]
            a = jnp.exp(log_a)
            th = jnp.tanh(log_a)
            u = jnp.sqrt(-2.0 * th / (1.0 - th)) * (i * xa)

            for s in (1, 2, 4):
                keep = sub >= s
                u = jnp.where(keep, a * pltpu.roll(u, s, 0) + u, u)
                a = jnp.where(keep, a * pltpu.roll(a, s, 0), a)
            a_scr[bi, :, cs] = a
            u_scr[bi, :, cs] = u

    def group(k, hs):
        off = pl.multiple_of(k * SUBLANES, SUBLANES)
        out = []
        for bi in range(nb):
            hg = a_scr[bi, pl.ds(off, SUBLANES), :] * hs[bi] + u_scr[bi, pl.ds(off, SUBLANES), :]
            u_scr[bi, pl.ds(off, SUBLANES), :] = hg
            out.append(hg[SUBLANES - 1:SUBLANES, :])
        return tuple(out)

    h_last = lax.fori_loop(0, rows // SUBLANES, group, tuple(hc_ref[bi] for bi in range(nb)))
    for bi in range(nb):
        hc_ref[bi] = h_last[bi]
        ho_ref[bi] = h_last[bi]
        y_ref[bi] = u_scr[bi] * gg_scr[bi]


def _lru_strided_kernel(x_ref, g_ref, w_ref, cw_ref, cb_ref, wax_ref, ba_ref, bx_ref, lam_ref, conv0_ref, h0_ref,
                        y_ref, convo_ref, ho_ref, tail_ref, hc_ref, *, width):
    W, G = width, LRU_GROUP
    R, nblk = x_ref.shape[0], x_ref.shape[1]

    @pl.when(pl.program_id(1) == 0)
    def _():
        tail_ref[...] = conv0_ref[...]
        hc_ref[...] = h0_ref[...]

    rowi = lax.broadcasted_iota(jnp.int32, (R, G), 0)
    row0 = rowi == 0
    sp = _softplus(-lam_ref[...])
    n = _rms(jnp.concatenate([x_ref[:, j, :] for j in range(nblk)], axis=0), g_ref[...]).astype(bf16)
    blk = lambda a, j: a[j * R:(j + 1) * R]

    for k in range(W // G):
        cs = slice(k * G, (k + 1) * G)
        xl = _dot(n, w_ref[:, cs])
        gg = _gelu_tanh(_dot(n, w_ref[:, W + k * G:W + (k + 1) * G]))
        tails = tail_ref[:, cs]
        cw = cw_ref[:, cs]
        prev_group = {j: jnp.where(row0, tails[j:j + 1, :], pltpu.roll(blk(xl, j), 1, 0))
                      for j in range(nblk - CONV_W + 1, nblk)}
        back = lambda j, s: blk(xl, j - s) if j >= s else prev_group[j - s + nblk]
        xa = jnp.concatenate(
            [cb_ref[:, cs] + cw[CONV_W - 1:CONV_W, :] * blk(xl, j)
             + sum(cw[CONV_W - 1 - s:CONV_W - s, :] * back(j, s) for s in range(1, CONV_W))
             for j in range(nblk)], axis=0)
        new_tail = jnp.concatenate([blk(xl, j)[R - 1:R, :] for j in range(nblk)], axis=0)
        tail_ref[:, cs] = new_tail
        convo_ref[:, cs] = new_tail

        pa = _dot(xa.astype(bf16), wax_ref[k])
        r = jax.nn.sigmoid(pa[:, :G] + ba_ref[:, cs])
        i = jax.nn.sigmoid(pa[:, G:] + bx_ref[:, cs])
        log_a = (-LRU_C) * r * sp[:, cs]
        a = jnp.exp(log_a)
        th = jnp.tanh(log_a)
        u = jnp.sqrt(-2.0 * th / (1.0 - th)) * (i * xa)

        acc_a, acc_u = [blk(a, 0)], [blk(u, 0)]
        for j in range(1, nblk):
            acc_u.append(blk(a, j) * acc_u[-1] + blk(u, j))
            acc_a.append(blk(a, j) * acc_a[-1])
        ga, gu = acc_a[-1], acc_u[-1]
        s = 1
        while s < R:
            keep = rowi >= s
            gu = jnp.where(keep, ga * pltpu.roll(gu, s, 0) + gu, gu)
            ga = jnp.where(keep, ga * pltpu.roll(ga, s, 0), ga)
            s *= 2
        h0 = hc_ref[:, cs]
        h_end = ga * h0 + gu
        h_in = jnp.where(row0, h0, pltpu.roll(h_end, 1, 0))
        hc_ref[:, cs] = h_end[R - 1:R, :]
        ho_ref[:, cs] = h_end[R - 1:R, :]
        for j in range(nblk):
            y_ref[:, j, cs] = (acc_a[j] * h_in + acc_u[j]) * blk(gg, j)


def _lru(x, wts, l, conv8, h0, tl):
    B, L, D = x.shape
    W = conv8.shape[-1]
    nb = SEQ_PER_STEP
    lmap = lambda b, t: (l, 0, 0)
    weights = (wts["mix_norm"], wts["w_all"], wts["lru_conv_w"], wts["lru_conv_b"], wts["lru_wax"], wts["lru_b_a"],
               wts["lru_b_x"], wts["lru_lambda"])
    weight_specs = [
        _resident((None, 1, D), lmap),
        _resident((None, D, 2 * W), lmap),
        _resident((None, CONV_W, W), lmap),
        _resident((None, 1, W), lmap),
        _resident((None, W // LRU_GROUP, LRU_GROUP, 2 * LRU_GROUP), lambda b, t: (l, 0, 0, 0)),
        _resident((None, 1, W), lmap),
        _resident((None, 1, W), lmap),
        _resident((None, 1, W), lmap),
    ]
    if tl % (SUBLANES * SUBLANES) == 0:
        r = tl // SUBLANES
        state = lambda rows: pl.BlockSpec((None, rows, W), lambda b, t: (b, 0, 0))
        y, conv_out, h_out = pl.pallas_call(
            functools.partial(_lru_strided_kernel, width=W),
            grid=(B, L // tl),
            in_specs=[pl.BlockSpec((None, r, SUBLANES, D), lambda b, t: (b, t, 0, 0))] + weight_specs
            + [state(SUBLANES), state(1)],
            out_specs=[pl.BlockSpec((None, r, SUBLANES, W), lambda b, t: (b, t, 0, 0)), state(SUBLANES), state(1)],
            out_shape=[
                jax.ShapeDtypeStruct((B, L // SUBLANES, SUBLANES, W), f32),
                jax.ShapeDtypeStruct((B, SUBLANES, W), f32),
                jax.ShapeDtypeStruct((B, 1, W), f32),
            ],
            scratch_shapes=[pltpu.VMEM((SUBLANES, W), f32), pltpu.VMEM((1, W), f32)],
            compiler_params=pltpu.CompilerParams(dimension_semantics=("arbitrary", "arbitrary"),
                                                 vmem_limit_bytes=VMEM_LIMIT),
            name="lru",
        )(x.reshape(B, L // SUBLANES, SUBLANES, D), *weights, conv8, h0)
        return y.reshape(B, L, W), conv_out, h_out
    return pl.pallas_call(
        functools.partial(_lru_kernel, width=W),
        grid=(B // nb, L // tl),
        in_specs=[
            pl.BlockSpec((nb, tl, D), lambda b, t: (b, t, 0)),
            _resident((None, 1, D), lmap),
            _resident((None, D, 2 * W), lmap),
            _resident((None, CONV_W, W), lmap),
            _resident((None, 1, W), lmap),
            _resident((None, W // LRU_GROUP, LRU_GROUP, 2 * LRU_GROUP), lambda b, t: (l, 0, 0, 0)),
            _resident((None, 1, W), lmap),
            _resident((None, 1, W), lmap),
            _resident((None, 1, W), lmap),
            pl.BlockSpec((nb, SUBLANES, W), lambda b, t: (b, 0, 0)),
            pl.BlockSpec((nb, 1, W), lambda b, t: (b, 0, 0)),
        ],
        out_specs=[
            pl.BlockSpec((nb, tl, W), lambda b, t: (b, t, 0)),
            pl.BlockSpec((nb, SUBLANES, W), lambda b, t: (b, 0, 0)),
            pl.BlockSpec((nb, 1, W), lambda b, t: (b, 0, 0)),
        ],
        out_shape=[
            jax.ShapeDtypeStruct((B, L, W), f32),
            jax.ShapeDtypeStruct((B, SUBLANES, W), f32),
            jax.ShapeDtypeStruct((B, 1, W), f32),
        ],
        scratch_shapes=[
            pltpu.VMEM((nb, SUBLANES, W), f32),
            pltpu.VMEM((nb, 1, W), f32),
            pltpu.VMEM((nb, tl, W), f32),
            pltpu.VMEM((nb, tl, W), f32),
            pltpu.VMEM((nb, tl, W), f32),
        ],
        compiler_params=pltpu.CompilerParams(dimension_semantics=("arbitrary", "arbitrary"),
                                             vmem_limit_bytes=VMEM_LIMIT),
        name="lru",
    )(x, wts["mix_norm"], wts["w_all"], wts["lru_conv_w"], wts["lru_conv_b"], wts["lru_wax"], wts["lru_b_a"],
      wts["lru_b_x"], wts["lru_lambda"], conv8, h0)


def _split3(x):
    a = x.astype(bf16)
    r = x - a.astype(f32)
    b = r.astype(bf16)
    return a, b, (r - b.astype(f32)).astype(bf16)


def _expand_heads(x, e3_ref):
    return _dot(jnp.concatenate(_split3(x), axis=1), e3_ref[...])


def _ssd_kernel(x_ref, g_ref, wz_ref, wx_ref, wdt_ref, wdte_ref, wdto_ref, cw_ref, cb_ref, dtb_ref, dtbe_ref, dtbo_ref,
                alog_ref, aloge_ref, alogo_ref, dskip_ref, ng_ref, e3_ref, conv0_ref, h0_ref,
                y_ref, convo_ref, ho_ref,
                tail_ref, hst_ref, y_scr, e_scr, st_scr, *, q):
    @pl.when(pl.program_id(1) == 0)
    def _():
        tail_ref[...] = conv0_ref[...]
        hst_ref[...] = h0_ref[...]

    for bi in range(x_ref.shape[0]):
        _ssd_tile(x_ref.at[bi], g_ref, wz_ref, wx_ref, wdt_ref, wdte_ref, wdto_ref, cw_ref, cb_ref, dtb_ref, dtbe_ref,
                  dtbo_ref, alog_ref, aloge_ref, alogo_ref, dskip_ref, ng_ref, e3_ref,
                  y_ref.at[bi], convo_ref.at[bi], ho_ref.at[bi],
                  tail_ref.at[bi], hst_ref.at[bi], y_scr.at[bi], e_scr.at[bi], st_scr.at[bi], q=q)


def _ssd_tile(x_ref, g_ref, wz_ref, wx_ref, wdt_ref, wdte_ref, wdto_ref, cw_ref, cb_ref, dtb_ref, dtbe_ref, dtbo_ref,
              alog_ref, aloge_ref, alogo_ref, dskip_ref, ng_ref, e3_ref,
              y_ref, convo_ref, ho_ref, tail_ref, hst_ref, y_scr, e_scr, st_scr, *, q):
    rows = x_ref.shape[0]
    nch = rows // q
    inner = SSD_HEADS * SSD_HEAD_DIM
    gn = SSD_GROUPS * SSD_STATE
    ppg = SSD_HEADS // 2 // SSD_GROUPS
    gw = inner // SSD_GROUPS

    n = _rms(x_ref[...], g_ref[...]).astype(bf16)
    z = _dot(n, wz_ref[...])
    xr = _dot(n, wx_ref[...])
    xc = _silu(_causal_conv(tail_ref, xr, cw_ref[...], cb_ref[...]))
    convo_ref[...] = xr[rows - SUBLANES:, :]
    xs = xc[:, 0:inner]
    bmb = xc[:, inner:inner + gn].astype(bf16)
    cmb = xc[:, inner + gn:inner + 2 * gn].astype(bf16)

    ri = lax.broadcasted_iota(jnp.int32, (rows, rows), 0)
    ci = lax.broadcasted_iota(jnp.int32, (rows, rows), 1)
    same_blk = (ri // q) == (ci // q)
    tri_bd = (same_blk & (ci <= ri)).astype(f32)
    tri_bd_t = (same_blk & (ri <= ci)).astype(f32)
    li = lax.broadcasted_iota(jnp.int32, (q, 2 * q), 0)
    lj = lax.broadcasted_iota(jnp.int32, (q, 2 * q), 1)
    causal2 = jnp.where(lj >= q, lj - q, lj) <= li
    lane_hi = lax.broadcasted_iota(jnp.int32, (q, LANES), 1) >= SSD_HEAD_DIM
    a_neg = -jnp.exp(alog_ref[...])
    a_neg_e = -jnp.exp(aloge_ref[...])
    a_neg_o = -jnp.exp(alogo_ref[...])

    dt = _softplus(_dot(n, wdt_ref[...]) + dtb_ref[...])
    cum = jnp.dot(tri_bd, dt * a_neg, precision=_HI, preferred_element_type=f32)
    last = jnp.concatenate([jnp.broadcast_to(cum[(c + 1) * q - 1:(c + 1) * q, :], (q, LANES)) for c in range(nch)],
                           axis=0)
    decay_end = jnp.exp(last - cum) * dt
    cum_x = _expand_heads(cum, e3_ref)
    y_scr[...] = dskip_ref[...] * xs
    e_scr[...] = jnp.exp(cum_x)
    xd = (xs * _expand_heads(decay_end, e3_ref)).astype(bf16)
    dt_e = _softplus(_dot_nt(wdte_ref[...], n) + dtbe_ref[...])
    dt_o = _softplus(_dot_nt(wdto_ref[...], n) + dtbo_ref[...])
    cum_e = jnp.dot(dt_e * a_neg_e, tri_bd_t, precision=_HI, preferred_element_type=f32)
    cum_o = jnp.dot(dt_o * a_neg_o, tri_bd_t, precision=_HI, preferred_element_type=f32)

    for c in range(nch):
        rc = slice(c * q, (c + 1) * q)
        dt_t2 = jnp.concatenate([dt_e[:, rc], dt_o[:, rc]], axis=1)
        cum_t2 = jnp.concatenate([cum_e[:, rc], cum_o[:, rc]], axis=1)
        for g in range(SSD_GROUPS):
            bgb = bmb[rc, g * SSD_STATE:(g + 1) * SSD_STATE]
            cgb = cmb[rc, g * SSD_STATE:(g + 1) * SSD_STATE]
            cb2 = _dot_nt(cgb, jnp.concatenate([bgb, bgb], axis=0))
            for pr in range(ppg):
                m = g * ppg + pr
                ps = slice(m * LANES, (m + 1) * LANES)
                cx = cum_x[rc, ps]
                if 2 * q != LANES:
                    cx = jnp.concatenate([cx[:, :q], cx[:, SSD_HEAD_DIM:SSD_HEAD_DIM + q]], axis=1)
                wm = cb2 * jnp.where(causal2, jnp.exp(cx - cum_t2[m:m + 1, :]), 0.0) * dt_t2[m:m + 1, :]
                xp = xs[rc, ps]
                x_bd = jnp.concatenate([jnp.where(lane_hi, 0.0, xp), jnp.where(lane_hi, xp, 0.0)], axis=0)
                y_scr[rc, ps] += _dot(wm.astype(bf16), x_bd.astype(bf16))
            gs = slice(g * gw, (g + 1) * gw)
            st_scr[c, :, gs] = _dot_tn(bgb, xd[rc, gs])

    for c in range(nch):
        rc = slice(c * q, (c + 1) * q)
        for g in range(SSD_GROUPS):
            gs = slice(g * gw, (g + 1) * gw)
            y_off = _dot(cmb[rc, g * SSD_STATE:(g + 1) * SSD_STATE], hst_ref[:, gs].astype(bf16))
            y_scr[rc, gs] += y_off * e_scr[rc, gs]
            hst_ref[:, gs] = e_scr[(c + 1) * q - 1:(c + 1) * q, gs] * hst_ref[:, gs] + st_scr[c, :, gs]
    ho_ref[...] = hst_ref[...]

    y = y_scr[...] * _silu(z)
    ng = ng_ref[...]
    for g in range(SSD_GROUPS):
        sl = slice(g * gw, (g + 1) * gw)
        y_ref[:, sl] = _rms(y[:, sl], ng[:, sl])


def _ssd(x, wts, l, conv8, h0t, tq, q):
    B, L, D = x.shape
    inner = SSD_HEADS * SSD_HEAD_DIM
    cdim = inner + 2 * SSD_GROUPS * SSD_STATE
    half = SSD_HEADS // 2
    nb = SEQ_PER_STEP
    lmap = lambda b, t: (l, 0, 0)
    return pl.pallas_call(
        functools.partial(_ssd_kernel, q=q),
        grid=(B // nb, L // tq),
        in_specs=[
            pl.BlockSpec((nb, tq, D), lambda b, t: (b, t, 0)),
            _resident((None, 1, D), lmap),
            _resident((None, D, inner), lambda b, t: (l, 0, wts["ssd_z_col"] // inner)),
            _resident((None, D, cdim), lambda b, t: (l, 0, wts["ssd_x_col"] // cdim)),
            _resident((None, D, LANES), lmap),
            _resident((None, half, D), lmap),
            _resident((None, half, D), lmap),
            _resident((None, CONV_W, cdim), lmap),
            _resident((None, 1, cdim), lmap),
            _resident((None, 1, LANES), lmap),
            _resident((None, half, 1), lmap),
            _resident((None, half, 1), lmap),
            _resident((None, 1, LANES), lmap),
            _resident((None, half, 1), lmap),
            _resident((None, half, 1), lmap),
            _resident((None, 1, inner), lmap),
            _resident((None, 1, inner), lmap),
            _resident((3 * LANES, inner), lambda b, t: (0, 0)),
            pl.BlockSpec((nb, SUBLANES, cdim), lambda b, t: (b, 0, 0)),
            pl.BlockSpec((nb, SSD_STATE, inner), lambda b, t: (b, 0, 0)),
        ],
        out_specs=[
            pl.BlockSpec((nb, tq, inner), lambda b, t: (b, t, 0)),
            pl.BlockSpec((nb, SUBLANES, cdim), lambda b, t: (b, 0, 0)),
            pl.BlockSpec((nb, SSD_STATE, inner), lambda b, t: (b, 0, 0)),
        ],
        out_shape=[
            jax.ShapeDtypeStruct((B, L, inner), f32),
            jax.ShapeDtypeStruct((B, SUBLANES, cdim), f32),
            jax.ShapeDtypeStruct((B, SSD_STATE, inner), f32),
        ],
        scratch_shapes=[
            pltpu.VMEM((nb, SUBLANES, cdim), f32),
            pltpu.VMEM((nb, SSD_STATE, inner), f32),
            pltpu.VMEM((nb, tq, inner), f32),
            pltpu.VMEM((nb, tq, inner), f32),
            pltpu.VMEM((nb, tq // q, SSD_STATE, inner), f32),
        ],
        compiler_params=pltpu.CompilerParams(dimension_semantics=("arbitrary", "arbitrary"),
                                             vmem_limit_bytes=VMEM_LIMIT),
        name="ssd",
    )(x, wts["mix_norm"], wts["w_all"], wts["w_all"], wts["w_dt"], wts["w_dt_e"], wts["w_dt_o"], wts["ssd_conv_w"],
      wts["ssd_conv_b"], wts["ssd_dt_bias"], wts["ssd_dt_bias_e"], wts["ssd_dt_bias_o"], wts["ssd_a_log"],
      wts["ssd_a_log_e"], wts["ssd_a_log_o"], wts["ssd_d"], wts["ssd_norm"], wts["ssd_expand3"], conv8, h0t)


def _attn_kernel(slope_ref, sink_ref, x_ref, g_ref, w_ref, pk_ref, pv_ref,
                 y_ref, ko_ref, vo_ref, kp_scr, vp_scr, *, pos0, sub):
    @pl.when(pl.program_id(1) == 0)
    def _():
        kp_scr[...] = pk_ref[...]
        vp_scr[...] = pv_ref[...]

    for bi in range(x_ref.shape[0]):
        _attn_tile(slope_ref, sink_ref, x_ref.at[bi], g_ref, w_ref, y_ref.at[bi], ko_ref.at[bi], vo_ref.at[bi],
                   kp_scr.at[bi], vp_scr.at[bi], pos0=pos0, sub=sub)


def _attn_tile(slope_ref, sink_ref, x_ref, g_ref, w_ref, y_ref, ko_ref, vo_ref, kp_scr, vp_scr, *, pos0, sub):
    rows = x_ref.shape[0]
    s_len = WINDOW + sub
    aw = N_HEADS * HEAD_DIM
    kw = N_KV_HEADS * HEAD_DIM
    gqa = N_HEADS // N_KV_HEADS
    t = pl.program_id(1)

    n = _rms(x_ref[...], g_ref[...]).astype(bf16)
    p = _dot(n, w_ref[...])
    qq = p[:, :aw] * (HEAD_DIM ** -0.5)
    kk = jnp.concatenate([kp_scr[...], p[:, aw:aw + kw]], axis=0)
    vv = jnp.concatenate([vp_scr[...], p[:, aw + kw:aw + 2 * kw]], axis=0)
    k_new = kk[rows:, :]
    v_new = vv[rows:, :]
    kp_scr[...] = k_new
    vp_scr[...] = v_new
    ko_ref[...] = k_new
    vo_ref[...] = v_new

    iq = lax.broadcasted_iota(jnp.int32, (sub, s_len), 0)
    ik = lax.broadcasted_iota(jnp.int32, (sub, s_len), 1)
    neg_dist = -jnp.abs(iq - (ik - WINDOW)).astype(f32)
    qc = iq // CHUNK
    kc = ik // CHUNK - WIN_CHUNKS
    in_band = (kc >= qc - WIN_CHUNKS) & (kc <= qc)
    lane_q = lax.broadcasted_iota(jnp.int32, (sub, LANES), 1) >= HEAD_DIM
    lane_k = lax.broadcasted_iota(jnp.int32, (WINDOW + rows, LANES), 1) >= HEAD_DIM

    k_both, v_lo, v_hi = [], [], []
    for kvh in range(N_KV_HEADS):
        tile = slice((kvh // 2) * LANES, (kvh // 2 + 1) * LANES)
        odd = kvh % 2 == 1
        k_own = jnp.where(lane_k, kk[:, tile], 0.0) if odd else jnp.where(lane_k, 0.0, kk[:, tile])
        v_own = jnp.where(lane_k, vv[:, tile], 0.0) if odd else jnp.where(lane_k, 0.0, vv[:, tile])
        v_other = pltpu.roll(v_own, HEAD_DIM, 1)
        k_both.append((k_own + pltpu.roll(k_own, HEAD_DIM, 1)).astype(bf16))
        v_lo.append((v_other if odd else v_own).astype(bf16))
        v_hi.append((v_own if odd else v_other).astype(bf16))

    for st in range(rows // sub):
        r0 = st * sub
        k_pos = ik + (pos0 - WINDOW + r0 + t * rows)
        bias_base = jnp.where(in_band & (k_pos >= 0), neg_dist, -MASK_DIST)
        for kvh in range(N_KV_HEADS):
            if gqa * sub > ATT_STACK_ROWS:
                for m in range(kvh * gqa // 2, (kvh + 1) * gqa // 2):
                    qp = qq[r0:r0 + sub, m * LANES:(m + 1) * LANES]
                    o_pair = jnp.zeros((sub, LANES), f32)
                    for half in range(2):
                        h = 2 * m + half
                        qh = (jnp.where(lane_q, qp, 0.0) if half else jnp.where(lane_q, 0.0, qp)).astype(bf16)
                        s = _dot_nt(qh, k_both[kvh][r0:r0 + s_len]) + slope_ref[h] * bias_base
                        mx = jnp.maximum(jnp.max(s, axis=-1, keepdims=True), sink_ref[h])
                        e = jnp.exp(s - mx)
                        den = jnp.sum(e, axis=-1, keepdims=True) + jnp.exp(sink_ref[h] - mx)
                        vh = (v_hi if half else v_lo)[kvh][r0:r0 + s_len]
                        o_pair = o_pair + _dot(e.astype(bf16), vh) * (1.0 / den)
                    y_ref[r0:r0 + sub, m * LANES:(m + 1) * LANES] = o_pair
                continue
            q_rows, bias_rows, sink_rows = [], [], []
            for j in range(gqa):
                h = kvh * gqa + j
                qp = qq[r0:r0 + sub, (h // 2) * LANES:(h // 2 + 1) * LANES]
                q_rows.append((jnp.where(lane_q, qp, 0.0) if h % 2 else jnp.where(lane_q, 0.0, qp)).astype(bf16))
                bias_rows.append(slope_ref[h] * bias_base)
                sink_rows.append(jnp.full((sub, 1), sink_ref[h], f32))
            sink_col = jnp.concatenate(sink_rows, axis=0)
            s = (_dot_nt(jnp.concatenate(q_rows, axis=0), k_both[kvh][r0:r0 + s_len])
                 + jnp.concatenate(bias_rows, axis=0))
            mx = jnp.maximum(jnp.max(s, axis=-1, keepdims=True), sink_col)
            e = jnp.exp(s - mx)
            inv = 1.0 / (jnp.sum(e, axis=-1, keepdims=True) + jnp.exp(sink_col - mx))
            eb = e.astype(bf16)
            blk = lambda a, j: a[j * sub:(j + 1) * sub]
            even = [j for j in range(gqa) if (kvh * gqa + j) % 2 == 0]
            odd = [j for j in range(gqa) if (kvh * gqa + j) % 2 == 1]
            o_lo = _dot(jnp.concatenate([blk(eb, j) for j in even], axis=0), v_lo[kvh][r0:r0 + s_len])
            o_hi = _dot(jnp.concatenate([blk(eb, j) for j in odd], axis=0), v_hi[kvh][r0:r0 + s_len])
            for i, (je, jo) in enumerate(zip(even, odd)):
                m = (kvh * gqa + je) // 2
                y_ref[r0:r0 + sub, m * LANES:(m + 1) * LANES] = (blk(o_lo, i) * blk(inv, je)
                                                                 + blk(o_hi, i) * blk(inv, jo))


def _attn(x, wts, l, slopes, sink, prev_k, prev_v, tq, pos0):
    B, L, D = x.shape
    aw = N_HEADS * HEAD_DIM
    kw = N_KV_HEADS * HEAD_DIM
    nb = SEQ_PER_STEP
    lmap = lambda b, t: (l, 0, 0)
    smem = pl.BlockSpec(memory_space=pltpu.SMEM)
    return pl.pallas_call(
        functools.partial(_attn_kernel, pos0=pos0, sub=min(tq, ATT_SUB)),
        grid=(B // nb, L // tq),
        in_specs=[
            smem, smem,
            pl.BlockSpec((nb, tq, D), lambda b, t: (b, t, 0)),
            _resident((None, 1, D), lmap),
            _resident((None, D, aw + 2 * kw), lmap),
            pl.BlockSpec((nb, WINDOW, kw), lambda b, t: (b, 0, 0)),
            pl.BlockSpec((nb, WINDOW, kw), lambda b, t: (b, 0, 0)),
        ],
        out_specs=[
            pl.BlockSpec((nb, tq, aw), lambda b, t: (b, t, 0)),
            pl.BlockSpec((nb, WINDOW, kw), lambda b, t: (b, 0, 0)),
            pl.BlockSpec((nb, WINDOW, kw), lambda b, t: (b, 0, 0)),
        ],
        out_shape=[
            jax.ShapeDtypeStruct((B, L, aw), f32),
            jax.ShapeDtypeStruct((B, WINDOW, kw), f32),
            jax.ShapeDtypeStruct((B, WINDOW, kw), f32),
        ],
        scratch_shapes=[pltpu.VMEM((nb, WINDOW, kw), f32), pltpu.VMEM((nb, WINDOW, kw), f32)],
        compiler_params=pltpu.CompilerParams(dimension_semantics=("arbitrary", "arbitrary"),
                                             vmem_limit_bytes=VMEM_LIMIT),
        name="attn",
    )(slopes, sink, x, wts["mix_norm"], wts["w_qkv"], prev_k, prev_v)


def _merge_kernel(x_ref, ya_ref, yb_ref, yc_ref, g_ref, wg_ref, wb_ref, wo_ref, o_ref):
    x = x_ref[...]
    D = x.shape[-1]
    n = _rms(x, g_ref[...]).astype(bf16)
    merged = jnp.zeros(x.shape, f32)
    for b, y_ref in enumerate((ya_ref, yb_ref, yc_ref)):
        bw = y_ref.shape[-1]
        gate = jax.nn.sigmoid(_dot(n, wg_ref[:, b * D:(b + 1) * D]))
        merged = merged + gate * _dot(y_ref[...].astype(bf16), wb_ref[b * bw:(b + 1) * bw, :])
    o_ref[...] = x + _dot(merged.astype(bf16), wo_ref[...])


def _merge(x, ya, yb, yc, wts, l):
    T, D = x.shape
    tm = min(T, 512)
    lmap = lambda i: (l, 0, 0)
    tok = lambda w: pl.BlockSpec((tm, w), lambda i: (i, 0))
    return pl.pallas_call(
        _merge_kernel,
        grid=(T // tm,),
        in_specs=[
            tok(D), tok(ya.shape[-1]), tok(yb.shape[-1]), tok(yc.shape[-1]),
            _resident((None, 1, D), lmap),
            _resident((None, D, 3 * D), lmap),
            _resident((None, wts["w_branch"].shape[1], D), lmap),
            _resident((None, D, D), lmap),
        ],
        out_specs=tok(D),
        out_shape=jax.ShapeDtypeStruct((T, D), f32),
        compiler_params=pltpu.CompilerParams(dimension_semantics=("parallel",), vmem_limit_bytes=VMEM_LIMIT),
        name="merge",
    )(x, ya, yb, yc, wts["mix_norm"], wts["w_gates"], wts["w_branch"], wts["w_out"])


def _block_diag_groups(w):
    nl, nb, bs, _ = w.shape
    per = LRU_GROUP // bs
    w = w.reshape(nl, nb // per, per, bs, bs)
    eye = jnp.eye(per, dtype=w.dtype)
    out = w[:, :, :, :, None, :] * eye[None, None, :, None, :, None]
    return out.reshape(nl, nb // per, LRU_GROUP, LRU_GROUP)


def _prep_weights(ffn_norm, ffn_w_gate, ffn_w_up, ffn_w_down, mix_norm, w_in, lru_conv_w, lru_conv_b, lru_w_a,
                  lru_b_a, lru_w_x, lru_b_x, lru_lambda, ssd_conv_w, ssd_conv_b, ssd_dt_bias, ssd_a_log, ssd_d,
                  ssd_norm, w_branch, w_out):
    nl, D = mix_norm.shape
    W = lru_conv_w.shape[-1]
    inner = SSD_HEADS * SSD_HEAD_DIM
    cdim = ssd_conv_w.shape[-1]
    aw = N_HEADS * HEAD_DIM
    kw = N_KV_HEADS * HEAD_DIM
    o = [0]
    for s in (W, W, inner, cdim, SSD_HEADS, aw, kw, kw, 3 * D):
        o.append(o[-1] + s)
    row = lambda a: a.reshape(nl, 1, -1)
    pad_lanes = lambda a: jnp.pad(a, [(0, 0)] * (a.ndim - 1) + [(0, LANES - a.shape[-1])])
    assert o[0] == 0 and o[2] % inner == 0 and o[3] % cdim == 0, "column ranges must be whole blocks"
    w_all = w_in.astype(bf16)
    w_dt = w_in[:, :, o[4]:o[5]]
    w_dt_t = jnp.swapaxes(w_dt, 1, 2).astype(bf16)
    col = lambda a: a[:, :, None]
    return {
        "ffn_norm": ffn_norm.reshape(nl, 2, 1, D),
        "ffn_w_gate": ffn_w_gate.astype(bf16), "ffn_w_up": ffn_w_up.astype(bf16), "ffn_w_down": ffn_w_down.astype(bf16),
        "mix_norm": row(mix_norm),
        "w_all": w_all, "ssd_z_col": o[2], "ssd_x_col": o[3],
        "w_dt": pad_lanes(w_dt).astype(bf16),
        "w_dt_e": w_dt_t[:, 0::2], "w_dt_o": w_dt_t[:, 1::2],
        "w_qkv": w_all[:, :, o[5]:o[8]],
        "w_gates": w_all[:, :, o[8]:o[9]],
        "lru_conv_w": lru_conv_w, "lru_conv_b": row(lru_conv_b),
        "lru_wax": jnp.concatenate([_block_diag_groups(lru_w_a), _block_diag_groups(lru_w_x)], axis=-1).astype(bf16),
        "lru_b_a": row(lru_b_a), "lru_b_x": row(lru_b_x), "lru_lambda": row(lru_lambda),
        "ssd_conv_w": ssd_conv_w, "ssd_conv_b": row(ssd_conv_b),
        "ssd_dt_bias": pad_lanes(row(ssd_dt_bias)),
        "ssd_dt_bias_e": col(ssd_dt_bias[:, 0::2]), "ssd_dt_bias_o": col(ssd_dt_bias[:, 1::2]),
        "ssd_a_log": pad_lanes(row(ssd_a_log)),
        "ssd_a_log_e": col(ssd_a_log[:, 0::2]), "ssd_a_log_o": col(ssd_a_log[:, 1::2]),
        "ssd_d": row(jnp.repeat(ssd_d, SSD_HEAD_DIM, axis=-1)),
        "ssd_norm": row(ssd_norm),
        "ssd_expand3": (jnp.arange(3 * LANES)[:, None] % LANES
                        == (jnp.arange(inner) // SSD_HEAD_DIM)[None, :]).astype(bf16),
        "w_branch": w_branch.astype(bf16), "w_out": w_out.astype(bf16),
    }


def _pad_conv_state(s):
    return jnp.pad(s, ((0, 0), (SUBLANES - s.shape[1], 0), (0, 0)))


def _layer(x, wts, l, slopes, sink, state, tiles, pos0, final_norm):
    B, L, D = x.shape
    cache_k, cache_v, lru_conv, lru_h, ssd_conv, ssd_h = state
    tl, tq_ssd, q, tq_att = tiles
    kw = N_KV_HEADS * HEAD_DIM
    inner = SSD_HEADS * SSD_HEAD_DIM
    ffn_w = (wts["ffn_norm"], wts["ffn_w_gate"], wts["ffn_w_up"], wts["ffn_w_down"])

    x = _ffn(x.reshape(B * L, D), *ffn_w, l, 0).reshape(B, L, D)
    ya, lru_conv8, lru_hn = _lru(x, wts, l, _pad_conv_state(lru_conv), lru_h[:, None, :], tl)
    ssd_ht = jnp.swapaxes(ssd_h.reshape(B, inner, SSD_STATE), 1, 2)
    yb, ssd_conv8, ssd_htn = _ssd(x, wts, l, _pad_conv_state(ssd_conv), ssd_ht, tq_ssd, q)
    ssd_hn = jnp.swapaxes(ssd_htn, 1, 2)
    yc, kn, vn = _attn(x, wts, l, slopes, sink, cache_k.reshape(B, WINDOW, kw), cache_v.reshape(B, WINDOW, kw),
                       tq_att, pos0)
    flat = lambda a: a.reshape(B * L, a.shape[-1])
    x = _merge(flat(x), flat(ya), flat(yb), flat(yc), wts, l)
    x = _ffn(x, *ffn_w, l, 1, final_norm=final_norm).reshape(B, L, D)
    new_state = (kn.reshape(B, WINDOW, N_KV_HEADS, HEAD_DIM), vn.reshape(B, WINDOW, N_KV_HEADS, HEAD_DIM),
                 lru_conv8[:, SUBLANES - (CONV_W - 1):], lru_hn[:, 0],
                 ssd_conv8[:, SUBLANES - (CONV_W - 1):], ssd_hn.reshape(B, SSD_HEADS, SSD_HEAD_DIM, SSD_STATE))
    return x, new_state


def kernel(x_prompt, x_sample, cache_attn_k, cache_attn_v, state_lru_conv, state_lru_h, state_ssd_conv, state_ssd_h, ffn_norm, ffn_w_gate, ffn_w_up, ffn_w_down, mix_norm, w_in, lru_conv_w, lru_conv_b, lru_w_a, lru_b_a, lru_w_x, lru_b_x, lru_lambda, ssd_conv_w, ssd_conv_b, ssd_dt_bias, ssd_a_log, ssd_d, ssd_norm, attn_sink, w_branch, w_out, final_norm):
    depth = mix_norm.shape[0]
    bp, lp, D = x_prompt.shape
    bs, ls, _ = x_sample.shape
    wts = _prep_weights(ffn_norm, ffn_w_gate, ffn_w_up, ffn_w_down, mix_norm, w_in, lru_conv_w, lru_conv_b, lru_w_a,
                        lru_b_a, lru_w_x, lru_b_x, lru_lambda, ssd_conv_w, ssd_conv_b, ssd_dt_bias, ssd_a_log, ssd_d,
                        ssd_norm, w_branch, w_out)
    slopes = jnp.exp2(-8.0 * (jnp.arange(N_HEADS, dtype=f32) + 1.0) / N_HEADS)
    fin = final_norm.reshape(1, D)
    kw = N_KV_HEADS * HEAD_DIM
    zero_state = (jnp.zeros((bp, WINDOW, kw), f32), jnp.zeros((bp, WINDOW, kw), f32),
                  jnp.zeros((bp, CONV_W - 1, lru_conv_w.shape[-1]), f32), jnp.zeros((bp, lru_conv_w.shape[-1]), f32),
                  jnp.zeros((bp, CONV_W - 1, ssd_conv_w.shape[-1]), f32),
                  jnp.zeros((bp, SSD_HEADS, SSD_HEAD_DIM, SSD_STATE), f32))
    tiles_p = (min(lp, 512), min(lp, 256), CHUNK, min(lp, 512))
    tiles_s = (ls, ls, ls, ls)
    xp, xs = x_prompt, x_sample
    p_new = [[] for _ in range(6)]
    s_new = [[] for _ in range(6)]
    for l in range(depth):
        last = fin if l == depth - 1 else None
        xp, st_p = _layer(xp, wts, l, slopes, attn_sink[l], zero_state, tiles_p, 0, last)
        st_s = (cache_attn_k[l], cache_attn_v[l], state_lru_conv[l], state_lru_h[l], state_ssd_conv[l], state_ssd_h[l])
        xs, st_sn = _layer(xs, wts, l, slopes, attn_sink[l], st_s, tiles_s, PAST_LEN, last)
        for j in range(6):
            p_new[j].append(st_p[j])
            s_new[j].append(st_sn[j])
    pk, pv, plc, plh, psc, psh = [jnp.stack(t, axis=0) for t in p_new]
    sk, sv, slc, slh, ssc, ssh = [jnp.stack(t, axis=0) for t in s_new]
    return (xp, xs, pk, pv, plc, plh, psc, psh, sk, sv, slc, slh, ssc, ssh)
```

```python
import functools
import math

import jax
import jax.numpy as jnp
from jax import lax
from jax.experimental import pallas as pl
from jax.experimental.pallas import tpu as pltpu

f32 = jnp.float32
bf16 = jnp.bfloat16

EPS = 1e-6
CHUNK = 64
CONV_W = 4
LRU_BLOCKS = 16
LRU_C = 8.0
LRU_GROUP = 256
SSD_HEADS = 16
SSD_HEAD_DIM = 64
SSD_GROUPS = 2
SSD_STATE = 128
N_HEADS = 16
N_KV_HEADS = 4
HEAD_DIM = 64
WINDOW = 128
WIN_CHUNKS = WINDOW // CHUNK
PAST_LEN = 4096
NEG = -1e30
MASK_DIST = -NEG * 2.0 ** 8
ATT_SUB = 128
ATT_STACK_ROWS = 64
SEQ_PER_STEP = 1
LANES = 128
SUBLANES = 8
VMEM_LIMIT = 56 * 1024 * 1024

_HI = lax.Precision.HIGHEST


def _rms(x, g):
    return x * lax.rsqrt(jnp.mean(x * x, axis=-1, keepdims=True) + EPS) * g


def _silu(x):
    return x * jax.nn.sigmoid(x)


def _softplus(x):
    return jnp.maximum(x, 0.0) + jnp.log1p(jnp.exp(-jnp.abs(x)))


def _gelu_tanh(x):
    c = math.sqrt(2.0 / math.pi)
    return x * (0.5 * (1.0 + jnp.tanh(c * (x + 0.044715 * (x * x * x)))))


def _dot(a, b):
    return jnp.dot(a, b, preferred_element_type=f32)


def _dot_nt(a, b):
    return lax.dot_general(a, b, (((1,), (1,)), ((), ())), preferred_element_type=f32)


def _dot_tn(a, b):
    return lax.dot_general(a, b, (((0,), (0,)), ((), ())), preferred_element_type=f32)


def _causal_conv(tail_ref, x, cw, cb):
    rows = x.shape[0]
    ext = jnp.concatenate([tail_ref[...], x], axis=0)
    y = cb + cw[CONV_W - 1:CONV_W, :] * x
    for s in range(1, CONV_W):
        y = y + cw[CONV_W - 1 - s:CONV_W - s, :] * pltpu.roll(ext, s, 0)[SUBLANES:SUBLANES + rows]
    tail_ref[...] = x[rows - SUBLANES:, :]
    return y


def _resident(shape, index_map):
    return pl.BlockSpec(shape, index_map, pipeline_mode=pl.Buffered(1))


def _ffn_kernel(*refs, n_chunks, cf, final):
    if final:
        x_ref, g_ref, wg_ref, wu_ref, wd_ref, fg_ref, o_ref, a_ref = refs
    else:
        x_ref, g_ref, wg_ref, wu_ref, wd_ref, o_ref, a_ref = refs
    x = x_ref[...]
    n = _rms(x, g_ref[...]).astype(bf16)
    for c in range(n_chunks):
        sl = slice(c * cf, (c + 1) * cf)
        gate = _dot(n, wg_ref[:, sl])
        up = _dot(n, wu_ref[:, sl])
        a_ref[:, sl] = (_silu(gate) * up).astype(bf16)
    y = x + 0.5 * _dot(a_ref[...], wd_ref[...])
    if final:
        y = _rms(y, fg_ref[...])
    o_ref[...] = y


def _ffn(x, norm, wg, wu, wd, l, j, final_norm=None):
    T, D = x.shape
    F = wg.shape[-1]
    tm = min(T, 1024)
    cf = 256
    final = final_norm is not None
    wmap = lambda i: (l, j, 0, 0)
    in_specs = [
        pl.BlockSpec((tm, D), lambda i: (i, 0)),
        _resident((None, None, 1, D), wmap),
        _resident((None, None, D, F), wmap),
        _resident((None, None, D, F), wmap),
        _resident((None, None, F, D), wmap),
    ]
    args = [x, norm, wg, wu, wd]
    if final:
        in_specs.append(_resident((1, D), lambda i: (0, 0)))
        args.append(final_norm)
    return pl.pallas_call(
        functools.partial(_ffn_kernel, n_chunks=F // cf, cf=cf, final=final),
        grid=(T // tm,),
        in_specs=in_specs,
        out_specs=pl.BlockSpec((tm, D), lambda i: (i, 0)),
        out_shape=jax.ShapeDtypeStruct((T, D), f32),
        scratch_shapes=[pltpu.VMEM((tm, F), bf16)],
        compiler_params=pltpu.CompilerParams(dimension_semantics=("parallel",), vmem_limit_bytes=VMEM_LIMIT),
        name="ffn_final" if final else "ffn",
    )(*args)


def _lru_kernel(x_ref, g_ref, w_ref, cw_ref, cb_ref, wax_ref, ba_ref, bx_ref, lam_ref, conv0_ref, h0_ref,
                y_ref, convo_ref, ho_ref, tail_ref, hc_ref, a_scr, u_scr, gg_scr, *, width):
    W = width
    nb, rows = x_ref.shape[0], x_ref.shape[1]

    @pl.when(pl.program_id(1) == 0)
    def _():
        tail_ref[...] = conv0_ref[...]
        hc_ref[...] = h0_ref[...]

    G = LRU_GROUP
    sub = lax.broadcasted_iota(jnp.int32, (rows, G), 0) & (SUBLANES - 1)
    sp = _softplus(-lam_ref[...])
    for bi in range(nb):
        n = _rms(x_ref[bi], g_ref[...]).astype(bf16)
        for k in range(W // G):
            cs = slice(k * G, (k + 1) * G)
            xl = _dot(n, w_ref[:, cs])
            gg_scr[bi, :, cs] = _gelu_tanh(_dot(n, w_ref[:, W + k * G:W + (k + 1) * G]))
            xa = _causal_conv(tail_ref.at[bi, :, cs], xl, cw_ref[:, cs], cb_ref[:, cs])
            convo_ref[bi, :, cs] = xl[rows - SUBLANES:, :]

            pa = _dot(xa.astype(bf16), wax_ref[k])
            r = jax.nn.sigmoid(pa[:, :G] + ba_ref[:, cs])
            i = jax.nn.sigmoid(pa[:, G:] + bx_ref[:, cs])
            log_a = (-LRU_C) * r * sp[:, cs]
            a = jnp.exp(log_a)
            th = jnp.tanh(log_a)
            u = jnp.sqrt(-2.0 * th / (1.0 - th)) * (i * xa)

            for s in (1, 2, 4):
                keep = sub >= s
                u = jnp.where(keep, a * pltpu.roll(u, s, 0) + u, u)
                a = jnp.where(keep, a * pltpu.roll(a, s, 0), a)
            a_scr[bi, :, cs] = a
            u_scr[bi, :, cs] = u

    def group(k, hs):
        off = pl.multiple_of(k * SUBLANES, SUBLANES)
        out = []
        for bi in range(nb):
            hg = a_scr[bi, pl.ds(off, SUBLANES), :] * hs[bi] + u_scr[bi, pl.ds(off, SUBLANES), :]
            u_scr[bi, pl.ds(off, SUBLANES), :] = hg
            out.append(hg[SUBLANES - 1:SUBLANES, :])
        return tuple(out)

    h_last = lax.fori_loop(0, rows // SUBLANES, group, tuple(hc_ref[bi] for bi in range(nb)))
    for bi in range(nb):
        hc_ref[bi] = h_last[bi]
        ho_ref[bi] = h_last[bi]
        y_ref[bi] = u_scr[bi] * gg_scr[bi]


def _lru_strided_kernel(x_ref, g_ref, w_ref, cw_ref, cb_ref, wax_ref, ba_ref, bx_ref, lam_ref, conv0_ref, h0_ref,
                        y_ref, convo_ref, ho_ref, tail_ref, hc_ref, *, width):
    W, G = width, LRU_GROUP
    R, nblk = x_ref.shape[0], x_ref.shape[1]

    @pl.when(pl.program_id(1) == 0)
    def _():
        tail_ref[...] = conv0_ref[...]
        hc_ref[...] = h0_ref[...]

    rowi = lax.broadcasted_iota(jnp.int32, (R, G), 0)
    row0 = rowi == 0
    sp = _softplus(-lam_ref[...])
    n = _rms(jnp.concatenate([x_ref[:, j, :] for j in range(nblk)], axis=0), g_ref[...]).astype(bf16)
    blk = lambda a, j: a[j * R:(j + 1) * R]

    for k in range(W // G):
        cs = slice(k * G, (k + 1) * G)
        xl = _dot(n, w_ref[:, cs])
        gg = _gelu_tanh(_dot(n, w_ref[:, W + k * G:W + (k + 1) * G]))
        tails = tail_ref[:, cs]
        cw = cw_ref[:, cs]
        prev_group = {j: jnp.where(row0, tails[j:j + 1, :], pltpu.roll(blk(xl, j), 1, 0))
                      for j in range(nblk - CONV_W + 1, nblk)}
        back = lambda j, s: blk(xl, j - s) if j >= s else prev_group[j - s + nblk]
        xa = jnp.concatenate(
            [cb_ref[:, cs] + cw[CONV_W - 1:CONV_W, :] * blk(xl, j)
             + sum(cw[CONV_W - 1 - s:CONV_W - s, :] * back(j, s) for s in range(1, CONV_W))
             for j in range(nblk)], axis=0)
        new_tail = jnp.concatenate([blk(xl, j)[R - 1:R, :] for j in range(nblk)], axis=0)
        tail_ref[:, cs] = new_tail
        convo_ref[:, cs] = new_tail

        pa = _dot(xa.astype(bf16), wax_ref[k])
        r = jax.nn.sigmoid(pa[:, :G] + ba_ref[:, cs])
        i = jax.nn.sigmoid(pa[:, G:] + bx_ref[:, cs])
        log_a = (-LRU_C) * r * sp[:, cs]
        a = jnp.exp(log_a)
        th = jnp.tanh(log_a)
        u = jnp.sqrt(-2.0 * th / (1.0 - th)) * (i * xa)

        acc_a, acc_u = [blk(a, 0)], [blk(u, 0)]
        for j in range(1, nblk):
            acc_u.append(blk(a, j) * acc_u[-1] + blk(u, j))
            acc_a.append(blk(a, j) * acc_a[-1])
        ga, gu = acc_a[-1], acc_u[-1]
        s = 1
        while s < R:
            keep = rowi >= s
            gu = jnp.where(keep, ga * pltpu.roll(gu, s, 0) + gu, gu)
            ga = jnp.where(keep, ga * pltpu.roll(ga, s, 0), ga)
            s *= 2
        h0 = hc_ref[:, cs]
        h_end = ga * h0 + gu
        h_in = jnp.where(row0, h0, pltpu.roll(h_end, 1, 0))
        hc_ref[:, cs] = h_end[R - 1:R, :]
        ho_ref[:, cs] = h_end[R - 1:R, :]
        for j in range(nblk):
            y_ref[:, j, cs] = (acc_a[j] * h_in + acc_u[j]) * blk(gg, j)


def _lru(x, wts, l, conv8, h0, tl):
    B, L, D = x.shape
    W = conv8.shape[-1]
    nb = SEQ_PER_STEP
    lmap = lambda b, t: (l, 0, 0)
    weights = (wts["mix_norm"], wts["w_all"], wts["lru_conv_w"], wts["lru_conv_b"], wts["lru_wax"], wts["lru_b_a"],
               wts["lru_b_x"], wts["lru_lambda"])
    weight_specs = [
        _resident((None, 1, D), lmap),
        _resident((None, D, 2 * W), lmap),
        _resident((None, CONV_W, W), lmap),
        _resident((None, 1, W), lmap),
        _resident((None, W // LRU_GROUP, LRU_GROUP, 2 * LRU_GROUP), lambda b, t: (l, 0, 0, 0)),
        _resident((None, 1, W), lmap),
        _resident((None, 1, W), lmap),
        _resident((None, 1, W), lmap),
    ]
    if tl % (SUBLANES * SUBLANES) == 0:
        r = tl // SUBLANES
        state = lambda rows: pl.BlockSpec((None, rows, W), lambda b, t: (b, 0, 0))
        y, conv_out, h_out = pl.pallas_call(
            functools.partial(_lru_strided_kernel, width=W),
            grid=(B, L // tl),
            in_specs=[pl.BlockSpec((None, r, SUBLANES, D), lambda b, t: (b, t, 0, 0))] + weight_specs
            + [state(SUBLANES), state(1)],
            out_specs=[pl.BlockSpec((None, r, SUBLANES, W), lambda b, t: (b, t, 0, 0)), state(SUBLANES), state(1)],
            out_shape=[
                jax.ShapeDtypeStruct((B, L // SUBLANES, SUBLANES, W), f32),
                jax.ShapeDtypeStruct((B, SUBLANES, W), f32),
                jax.ShapeDtypeStruct((B, 1, W), f32),
            ],
            scratch_shapes=[pltpu.VMEM((SUBLANES, W), f32), pltpu.VMEM((1, W), f32)],
            compiler_params=pltpu.CompilerParams(dimension_semantics=("arbitrary", "arbitrary"),
                                                 vmem_limit_bytes=VMEM_LIMIT),
            name="lru",
        )(x.reshape(B, L // SUBLANES, SUBLANES, D), *weights, conv8, h0)
        return y.reshape(B, L, W), conv_out, h_out
    return pl.pallas_call(
        functools.partial(_lru_kernel, width=W),
        grid=(B // nb, L // tl),
        in_specs=[
            pl.BlockSpec((nb, tl, D), lambda b, t: (b, t, 0)),
            _resident((None, 1, D), lmap),
            _resident((None, D, 2 * W), lmap),
            _resident((None, CONV_W, W), lmap),
            _resident((None, 1, W), lmap),
            _resident((None, W // LRU_GROUP, LRU_GROUP, 2 * LRU_GROUP), lambda b, t: (l, 0, 0, 0)),
            _resident((None, 1, W), lmap),
            _resident((None, 1, W), lmap),
            _resident((None, 1, W), lmap),
            pl.BlockSpec((nb, SUBLANES, W), lambda b, t: (b, 0, 0)),
            pl.BlockSpec((nb, 1, W), lambda b, t: (b, 0, 0)),
        ],
        out_specs=[
            pl.BlockSpec((nb, tl, W), lambda b, t: (b, t, 0)),
            pl.BlockSpec((nb, SUBLANES, W), lambda b, t: (b, 0, 0)),
            pl.BlockSpec((nb, 1, W), lambda b, t: (b, 0, 0)),
        ],
        out_shape=[
            jax.ShapeDtypeStruct((B, L, W), f32),
            jax.ShapeDtypeStruct((B, SUBLANES, W), f32),
            jax.ShapeDtypeStruct((B, 1, W), f32),
        ],
        scratch_shapes=[
            pltpu.VMEM((nb, SUBLANES, W), f32),
            pltpu.VMEM((nb, 1, W), f32),
            pltpu.VMEM((nb, tl, W), f32),
            pltpu.VMEM((nb, tl, W), f32),
            pltpu.VMEM((nb, tl, W), f32),
        ],
        compiler_params=pltpu.CompilerParams(dimension_semantics=("arbitrary", "arbitrary"),
                                             vmem_limit_bytes=VMEM_LIMIT),
        name="lru",
    )(x, wts["mix_norm"], wts["w_all"], wts["lru_conv_w"], wts["lru_conv_b"], wts["lru_wax"], wts["lru_b_a"],
      wts["lru_b_x"], wts["lru_lambda"], conv8, h0)


def _split3(x):
    a = x.astype(bf16)
    r = x - a.astype(f32)
    b = r.astype(bf16)
    return a, b, (r - b.astype(f32)).astype(bf16)


def _expand_heads(x, e3_ref):
    return _dot(jnp.concatenate(_split3(x), axis=1), e3_ref[...])


def _ssd_kernel(x_ref, g_ref, wz_ref, wx_ref, wdt_ref, wdte_ref, wdto_ref, cw_ref, cb_ref, dtb_ref, dtbe_ref, dtbo_ref,
                alog_ref, aloge_ref, alogo_ref, dskip_ref, ng_ref, e3_ref, conv0_ref, h0_ref,
                y_ref, convo_ref, ho_ref,
                tail_ref, hst_ref, y_scr, e_scr, st_scr, *, q):
    @pl.when(pl.program_id(1) == 0)
    def _():
        tail_ref[...] = conv0_ref[...]
        for bi in range(x_ref.shape[0]):
            hst_ref[bi] = h0_ref[bi].T

    for bi in range(x_ref.shape[0]):
        _ssd_tile(x_ref.at[bi], g_ref, wz_ref, wx_ref, wdt_ref, wdte_ref, wdto_ref, cw_ref, cb_ref, dtb_ref, dtbe_ref,
                  dtbo_ref, alog_ref, aloge_ref, alogo_ref, dskip_ref, ng_ref, e3_ref,
                  y_ref.at[bi], convo_ref.at[bi], ho_ref.at[bi],
                  tail_ref.at[bi], hst_ref.at[bi], y_scr.at[bi], e_scr.at[bi], st_scr.at[bi], q=q)


def _ssd_tile(x_ref, g_ref, wz_ref, wx_ref, wdt_ref, wdte_ref, wdto_ref, cw_ref, cb_ref, dtb_ref, dtbe_ref, dtbo_ref,
              alog_ref, aloge_ref, alogo_ref, dskip_ref, ng_ref, e3_ref,
              y_ref, convo_ref, ho_ref, tail_ref, hst_ref, y_scr, e_scr, st_scr, *, q):
    rows = x_ref.shape[0]
    nch = rows // q
    inner = SSD_HEADS * SSD_HEAD_DIM
    gn = SSD_GROUPS * SSD_STATE
    ppg = SSD_HEADS // 2 // SSD_GROUPS
    gw = inner // SSD_GROUPS

    n = _rms(x_ref[...], g_ref[...]).astype(bf16)
    z = _dot(n, wz_ref[...])
    xr = _dot(n, wx_ref[...])
    xc = _silu(_causal_conv(tail_ref, xr, cw_ref[...], cb_ref[...]))
    convo_ref[...] = xr[rows - SUBLANES:, :]
    xs = xc[:, 0:inner]
    bmb = xc[:, inner:inner + gn].astype(bf16)
    cmb = xc[:, inner + gn:inner + 2 * gn].astype(bf16)

    ri = lax.broadcasted_iota(jnp.int32, (rows, rows), 0)
    ci = lax.broadcasted_iota(jnp.int32, (rows, rows), 1)
    same_blk = (ri // q) == (ci // q)
    tri_bd = (same_blk & (ci <= ri)).astype(f32)
    tri_bd_t = (same_blk & (ri <= ci)).astype(f32)
    li = lax.broadcasted_iota(jnp.int32, (q, 2 * q), 0)
    lj = lax.broadcasted_iota(jnp.int32, (q, 2 * q), 1)
    causal2 = jnp.where(lj >= q, lj - q, lj) <= li
    lane_hi = lax.broadcasted_iota(jnp.int32, (q, LANES), 1) >= SSD_HEAD_DIM
    a_neg = -jnp.exp(alog_ref[...])
    a_neg_e = -jnp.exp(aloge_ref[...])
    a_neg_o = -jnp.exp(alogo_ref[...])

    dt = _softplus(_dot(n, wdt_ref[...]) + dtb_ref[...])
    cum = jnp.dot(tri_bd, dt * a_neg, precision=_HI, preferred_element_type=f32)
    last = jnp.concatenate([jnp.broadcast_to(cum[(c + 1) * q - 1:(c + 1) * q, :], (q, LANES)) for c in range(nch)],
                           axis=0)
    decay_end = jnp.exp(last - cum) * dt
    cum_x = _expand_heads(cum, e3_ref)
    y_scr[...] = dskip_ref[...] * xs
    e_scr[...] = jnp.exp(cum_x)
    xd = (xs * _expand_heads(decay_end, e3_ref)).astype(bf16)
    dt_e = _softplus(_dot_nt(wdte_ref[...], n) + dtbe_ref[...])
    dt_o = _softplus(_dot_nt(wdto_ref[...], n) + dtbo_ref[...])
    cum_e = jnp.dot(dt_e * a_neg_e, tri_bd_t, precision=_HI, preferred_element_type=f32)
    cum_o = jnp.dot(dt_o * a_neg_o, tri_bd_t, precision=_HI, preferred_element_type=f32)

    for c in range(nch):
        rc = slice(c * q, (c + 1) * q)
        dt_t2 = jnp.concatenate([dt_e[:, rc], dt_o[:, rc]], axis=1)
        cum_t2 = jnp.concatenate([cum_e[:, rc], cum_o[:, rc]], axis=1)
        for g in range(SSD_GROUPS):
            bgb = bmb[rc, g * SSD_STATE:(g + 1) * SSD_STATE]
            cgb = cmb[rc, g * SSD_STATE:(g + 1) * SSD_STATE]
            cb2 = _dot_nt(cgb, jnp.concatenate([bgb, bgb], axis=0))
            for pr in range(ppg):
                m = g * ppg + pr
                ps = slice(m * LANES, (m + 1) * LANES)
                cx = cum_x[rc, ps]
                if 2 * q != LANES:
                    cx = jnp.concatenate([cx[:, :q], cx[:, SSD_HEAD_DIM:SSD_HEAD_DIM + q]], axis=1)
                wm = cb2 * jnp.where(causal2, jnp.exp(cx - cum_t2[m:m + 1, :]), 0.0) * dt_t2[m:m + 1, :]
                xp = xs[rc, ps]
                x_bd = jnp.concatenate([jnp.where(lane_hi, 0.0, xp), jnp.where(lane_hi, xp, 0.0)], axis=0)
                y_scr[rc, ps] += _dot(wm.astype(bf16), x_bd.astype(bf16))
            gs = slice(g * gw, (g + 1) * gw)
            st_scr[c, :, gs] = _dot_tn(bgb, xd[rc, gs])

    for c in range(nch):
        rc = slice(c * q, (c + 1) * q)
        for g in range(SSD_GROUPS):
            gs = slice(g * gw, (g + 1) * gw)
            y_off = _dot(cmb[rc, g * SSD_STATE:(g + 1) * SSD_STATE], hst_ref[:, gs].astype(bf16))
            y_scr[rc, gs] += y_off * e_scr[rc, gs]
            hst_ref[:, gs] = e_scr[(c + 1) * q - 1:(c + 1) * q, gs] * hst_ref[:, gs] + st_scr[c, :, gs]

    y = y_scr[...] * _silu(z)
    ng = ng_ref[...]
    for g in range(SSD_GROUPS):
        sl = slice(g * gw, (g + 1) * gw)
        y_ref[:, sl] = _rms(y[:, sl], ng[:, sl])

    @pl.when(pl.program_id(1) == pl.num_programs(1) - 1)
    def _():
        ho_ref[...] = hst_ref[...].T


def _ssd(x, wts, l, conv8, h0, tq, q):
    B, L, D = x.shape
    inner = SSD_HEADS * SSD_HEAD_DIM
    cdim = inner + 2 * SSD_GROUPS * SSD_STATE
    half = SSD_HEADS // 2
    nb = SEQ_PER_STEP
    lmap = lambda b, t: (l, 0, 0)
    return pl.pallas_call(
        functools.partial(_ssd_kernel, q=q),
        grid=(B // nb, L // tq),
        in_specs=[
            pl.BlockSpec((nb, tq, D), lambda b, t: (b, t, 0)),
            _resident((None, 1, D), lmap),
            _resident((None, D, inner), lambda b, t: (l, 0, wts["ssd_z_col"] // inner)),
            _resident((None, D, cdim), lambda b, t: (l, 0, wts["ssd_x_col"] // cdim)),
            _resident((None, D, LANES), lmap),
            _resident((None, half, D), lmap),
            _resident((None, half, D), lmap),
            _resident((None, CONV_W, cdim), lmap),
            _resident((None, 1, cdim), lmap),
            _resident((None, 1, LANES), lmap),
            _resident((None, half, 1), lmap),
            _resident((None, half, 1), lmap),
            _resident((None, 1, LANES), lmap),
            _resident((None, half, 1), lmap),
            _resident((None, half, 1), lmap),
            _resident((None, 1, inner), lmap),
            _resident((None, 1, inner), lmap),
            _resident((3 * LANES, inner), lambda b, t: (0, 0)),
            pl.BlockSpec((nb, SUBLANES, cdim), lambda b, t: (b, 0, 0)),
            pl.BlockSpec((nb, inner, SSD_STATE), lambda b, t: (b, 0, 0)),
        ],
        out_specs=[
            pl.BlockSpec((nb, tq, inner), lambda b, t: (b, t, 0)),
            pl.BlockSpec((nb, SUBLANES, cdim), lambda b, t: (b, 0, 0)),
            pl.BlockSpec((nb, inner, SSD_STATE), lambda b, t: (b, 0, 0)),
        ],
        out_shape=[
            jax.ShapeDtypeStruct((B, L, inner), f32),
            jax.ShapeDtypeStruct((B, SUBLANES, cdim), f32),
            jax.ShapeDtypeStruct((B, inner, SSD_STATE), f32),
        ],
        scratch_shapes=[
            pltpu.VMEM((nb, SUBLANES, cdim), f32),
            pltpu.VMEM((nb, SSD_STATE, inner), f32),
            pltpu.VMEM((nb, tq, inner), f32),
            pltpu.VMEM((nb, tq, inner), f32),
            pltpu.VMEM((nb, tq // q, SSD_STATE, inner), f32),
        ],
        compiler_params=pltpu.CompilerParams(dimension_semantics=("arbitrary", "arbitrary"),
                                             vmem_limit_bytes=VMEM_LIMIT),
        name="ssd",
    )(x, wts["mix_norm"], wts["w_all"], wts["w_all"], wts["w_dt"], wts["w_dt_e"], wts["w_dt_o"], wts["ssd_conv_w"],
      wts["ssd_conv_b"], wts["ssd_dt_bias"], wts["ssd_dt_bias_e"], wts["ssd_dt_bias_o"], wts["ssd_a_log"],
      wts["ssd_a_log_e"], wts["ssd_a_log_o"], wts["ssd_d"], wts["ssd_norm"], wts["ssd_expand3"], conv8, h0)


def _attn_kernel(slope_ref, sink_ref, x_ref, g_ref, w_ref, pk_ref, pv_ref,
                 y_ref, ko_ref, vo_ref, kp_scr, vp_scr, *, pos0, sub):
    @pl.when(pl.program_id(1) == 0)
    def _():
        kp_scr[...] = pk_ref[...]
        vp_scr[...] = pv_ref[...]

    for bi in range(x_ref.shape[0]):
        _attn_tile(slope_ref, sink_ref, x_ref.at[bi], g_ref, w_ref, y_ref.at[bi], ko_ref.at[bi], vo_ref.at[bi],
                   kp_scr.at[bi], vp_scr.at[bi], pos0=pos0, sub=sub)


def _attn_tile(slope_ref, sink_ref, x_ref, g_ref, w_ref, y_ref, ko_ref, vo_ref, kp_scr, vp_scr, *, pos0, sub):
    rows = x_ref.shape[0]
    s_len = WINDOW + sub
    aw = N_HEADS * HEAD_DIM
    kw = N_KV_HEADS * HEAD_DIM
    gqa = N_HEADS // N_KV_HEADS
    t = pl.program_id(1)

    n = _rms(x_ref[...], g_ref[...]).astype(bf16)
    p = _dot(n, w_ref[...])
    qq = p[:, :aw] * (HEAD_DIM ** -0.5)
    kk = jnp.concatenate([kp_scr[...], p[:, aw:aw + kw]], axis=0)
    vv = jnp.concatenate([vp_scr[...], p[:, aw + kw:aw + 2 * kw]], axis=0)
    k_new = kk[rows:, :]
    v_new = vv[rows:, :]
    kp_scr[...] = k_new
    vp_scr[...] = v_new
    ko_ref[...] = k_new
    vo_ref[...] = v_new

    iq = lax.broadcasted_iota(jnp.int32, (sub, s_len), 0)
    ik = lax.broadcasted_iota(jnp.int32, (sub, s_len), 1)
    neg_dist = -jnp.abs(iq - (ik - WINDOW)).astype(f32)
    qc = iq // CHUNK
    kc = ik // CHUNK - WIN_CHUNKS
    in_band = (kc >= qc - WIN_CHUNKS) & (kc <= qc)
    lane_q = lax.broadcasted_iota(jnp.int32, (sub, LANES), 1) >= HEAD_DIM
    lane_k = lax.broadcasted_iota(jnp.int32, (WINDOW + rows, LANES), 1) >= HEAD_DIM

    k_both, v_lo, v_hi = [], [], []
    for kvh in range(N_KV_HEADS):
        tile = slice((kvh // 2) * LANES, (kvh // 2 + 1) * LANES)
        odd = kvh % 2 == 1
        k_own = jnp.where(lane_k, kk[:, tile], 0.0) if odd else jnp.where(lane_k, 0.0, kk[:, tile])
        v_own = jnp.where(lane_k, vv[:, tile], 0.0) if odd else jnp.where(lane_k, 0.0, vv[:, tile])
        v_other = pltpu.roll(v_own, HEAD_DIM, 1)
        k_both.append((k_own + pltpu.roll(k_own, HEAD_DIM, 1)).astype(bf16))
        v_lo.append((v_other if odd else v_own).astype(bf16))
        v_hi.append((v_own if odd else v_other).astype(bf16))

    for st in range(rows // sub):
        r0 = st * sub
        k_pos = ik + (pos0 - WINDOW + r0 + t * rows)
        bias_base = jnp.where(in_band & (k_pos >= 0), neg_dist, -MASK_DIST)
        for kvh in range(N_KV_HEADS):
            if gqa * sub > ATT_STACK_ROWS:
                for m in range(kvh * gqa // 2, (kvh + 1) * gqa // 2):
                    qp = qq[r0:r0 + sub, m * LANES:(m + 1) * LANES]
                    o_pair = jnp.zeros((sub, LANES), f32)
                    for half in range(2):
                        h = 2 * m + half
                        qh = (jnp.where(lane_q, qp, 0.0) if half else jnp.where(lane_q, 0.0, qp)).astype(bf16)
                        s = _dot_nt(qh, k_both[kvh][r0:r0 + s_len]) + slope_ref[h] * bias_base
                        mx = jnp.maximum(jnp.max(s, axis=-1, keepdims=True), sink_ref[h])
                        e = jnp.exp(s - mx)
                        den = jnp.sum(e, axis=-1, keepdims=True) + jnp.exp(sink_ref[h] - mx)
                        vh = (v_hi if half else v_lo)[kvh][r0:r0 + s_len]
                        o_pair = o_pair + _dot(e.astype(bf16), vh) * (1.0 / den)
                    y_ref[r0:r0 + sub, m * LANES:(m + 1) * LANES] = o_pair
                continue
            q_rows, bias_rows, sink_rows = [], [], []
            for j in range(gqa):
                h = kvh * gqa + j
                qp = qq[r0:r0 + sub, (h // 2) * LANES:(h // 2 + 1) * LANES]
                q_rows.append((jnp.where(lane_q, qp, 0.0) if h % 2 else jnp.where(lane_q, 0.0, qp)).astype(bf16))
                bias_rows.append(slope_ref[h] * bias_base)
                sink_rows.append(jnp.full((sub, 1), sink_ref[h], f32))
            sink_col = jnp.concatenate(sink_rows, axis=0)
            s = (_dot_nt(jnp.concatenate(q_rows, axis=0), k_both[kvh][r0:r0 + s_len])
                 + jnp.concatenate(bias_rows, axis=0))
            mx = jnp.maximum(jnp.max(s, axis=-1, keepdims=True), sink_col)
            e = jnp.exp(s - mx)
            inv = 1.0 / (jnp.sum(e, axis=-1, keepdims=True) + jnp.exp(sink_col - mx))
            eb = e.astype(bf16)
            blk = lambda a, j: a[j * sub:(j + 1) * sub]
            even = [j for j in range(gqa) if (kvh * gqa + j) % 2 == 0]
            odd = [j for j in range(gqa) if (kvh * gqa + j) % 2 == 1]
            o_lo = _dot(jnp.concatenate([blk(eb, j) for j in even], axis=0), v_lo[kvh][r0:r0 + s_len])
            o_hi = _dot(jnp.concatenate([blk(eb, j) for j in odd], axis=0), v_hi[kvh][r0:r0 + s_len])
            for i, (je, jo) in enumerate(zip(even, odd)):
                m = (kvh * gqa + je) // 2
                y_ref[r0:r0 + sub, m * LANES:(m + 1) * LANES] = (blk(o_lo, i) * blk(inv, je)
                                                                 + blk(o_hi, i) * blk(inv, jo))


def _attn(x, wts, l, slopes, sink, prev_k, prev_v, tq, pos0):
    B, L, D = x.shape
    aw = N_HEADS * HEAD_DIM
    kw = N_KV_HEADS * HEAD_DIM
    nb = SEQ_PER_STEP
    lmap = lambda b, t: (l, 0, 0)
    smem = pl.BlockSpec(memory_space=pltpu.SMEM)
    return pl.pallas_call(
        functools.partial(_attn_kernel, pos0=pos0, sub=min(tq, ATT_SUB)),
        grid=(B // nb, L // tq),
        in_specs=[
            smem, smem,
            pl.BlockSpec((nb, tq, D), lambda b, t: (b, t, 0)),
            _resident((None, 1, D), lmap),
            _resident((None, D, aw + 2 * kw), lambda b, t: (l, 0, wts["qkv_col"] // (aw + 2 * kw))),
            pl.BlockSpec((nb, WINDOW, kw), lambda b, t: (b, 0, 0)),
            pl.BlockSpec((nb, WINDOW, kw), lambda b, t: (b, 0, 0)),
        ],
        out_specs=[
            pl.BlockSpec((nb, tq, aw), lambda b, t: (b, t, 0)),
            pl.BlockSpec((nb, WINDOW, kw), lambda b, t: (b, 0, 0)),
            pl.BlockSpec((nb, WINDOW, kw), lambda b, t: (b, 0, 0)),
        ],
        out_shape=[
            jax.ShapeDtypeStruct((B, L, aw), f32),
            jax.ShapeDtypeStruct((B, WINDOW, kw), f32),
            jax.ShapeDtypeStruct((B, WINDOW, kw), f32),
        ],
        scratch_shapes=[pltpu.VMEM((nb, WINDOW, kw), f32), pltpu.VMEM((nb, WINDOW, kw), f32)],
        compiler_params=pltpu.CompilerParams(dimension_semantics=("arbitrary", "arbitrary"),
                                             vmem_limit_bytes=VMEM_LIMIT),
        name="attn",
    )(slopes, sink, x, wts["mix_norm"], wts["w_all"], prev_k, prev_v)


def _merge_kernel(x_ref, ya_ref, yb_ref, yc_ref, g_ref, wg_ref, wb_ref, wo_ref, o_ref):
    x = x_ref[...]
    D = x.shape[-1]
    n = _rms(x, g_ref[...]).astype(bf16)
    merged = jnp.zeros(x.shape, f32)
    for b, y_ref in enumerate((ya_ref, yb_ref, yc_ref)):
        bw = y_ref.shape[-1]
        gate = jax.nn.sigmoid(_dot(n, wg_ref[:, b * D:(b + 1) * D]))
        merged = merged + gate * _dot(y_ref[...].astype(bf16), wb_ref[b * bw:(b + 1) * bw, :])
    o_ref[...] = x + _dot(merged.astype(bf16), wo_ref[...])


def _merge(x, ya, yb, yc, wts, l):
    T, D = x.shape
    tm = min(T, 512)
    lmap = lambda i: (l, 0, 0)
    tok = lambda w: pl.BlockSpec((tm, w), lambda i: (i, 0))
    return pl.pallas_call(
        _merge_kernel,
        grid=(T // tm,),
        in_specs=[
            tok(D), tok(ya.shape[-1]), tok(yb.shape[-1]), tok(yc.shape[-1]),
            _resident((None, 1, D), lmap),
            _resident((None, D, 3 * D), lambda i: (l, 0, wts["gates_col"] // (3 * D))),
            _resident((None, wts["w_branch"].shape[1], D), lmap),
            _resident((None, D, D), lmap),
        ],
        out_specs=tok(D),
        out_shape=jax.ShapeDtypeStruct((T, D), f32),
        compiler_params=pltpu.CompilerParams(dimension_semantics=("parallel",), vmem_limit_bytes=VMEM_LIMIT),
        name="merge",
    )(x, ya, yb, yc, wts["mix_norm"], wts["w_all"], wts["w_branch"], wts["w_out"])


def _block_diag_groups(w):
    nl, nb, bs, _ = w.shape
    per = LRU_GROUP // bs
    w = w.reshape(nl, nb // per, per, bs, bs)
    eye = jnp.eye(per, dtype=w.dtype)
    out = w[:, :, :, :, None, :] * eye[None, None, :, None, :, None]
    return out.reshape(nl, nb // per, LRU_GROUP, LRU_GROUP)


def _prep_weights(ffn_norm, ffn_w_gate, ffn_w_up, ffn_w_down, mix_norm, w_in, lru_conv_w, lru_conv_b, lru_w_a,
                  lru_b_a, lru_w_x, lru_b_x, lru_lambda, ssd_conv_w, ssd_conv_b, ssd_dt_bias, ssd_a_log, ssd_d,
                  ssd_norm, w_branch, w_out):
    nl, D = mix_norm.shape
    W = lru_conv_w.shape[-1]
    inner = SSD_HEADS * SSD_HEAD_DIM
    cdim = ssd_conv_w.shape[-1]
    aw = N_HEADS * HEAD_DIM
    kw = N_KV_HEADS * HEAD_DIM
    o = [0]
    for s in (W, W, inner, cdim, SSD_HEADS, aw, kw, kw, 3 * D):
        o.append(o[-1] + s)
    row = lambda a: a.reshape(nl, 1, -1)
    pad_lanes = lambda a: jnp.pad(a, [(0, 0)] * (a.ndim - 1) + [(0, LANES - a.shape[-1])])
    w_dt = w_in[:, :, o[4]:o[5]]
    w_all = jnp.concatenate(
        [w_in[:, :, :o[4]], w_in[:, :, o[5]:], w_dt, jnp.zeros((nl, D, (-w_in.shape[-1]) % LANES), w_in.dtype)],
        axis=-1).astype(bf16)
    qkv_col, gates_col = o[5] - SSD_HEADS, o[8] - SSD_HEADS
    assert o[2] % inner == 0 and o[3] % cdim == 0 and qkv_col % (aw + 2 * kw) == 0 and gates_col % (3 * D) == 0, \
        "column ranges must be whole blocks"
    w_dt_t = jnp.swapaxes(w_dt, 1, 2).astype(bf16)
    col = lambda a: a[:, :, None]
    return {
        "ffn_norm": ffn_norm.reshape(nl, 2, 1, D),
        "ffn_w_gate": ffn_w_gate.astype(bf16), "ffn_w_up": ffn_w_up.astype(bf16), "ffn_w_down": ffn_w_down.astype(bf16),
        "mix_norm": row(mix_norm),
        "w_all": w_all, "ssd_z_col": o[2], "ssd_x_col": o[3], "qkv_col": qkv_col, "gates_col": gates_col,
        "w_dt": pad_lanes(w_dt).astype(bf16),
        "w_dt_e": w_dt_t[:, 0::2], "w_dt_o": w_dt_t[:, 1::2],
        "lru_conv_w": lru_conv_w, "lru_conv_b": row(lru_conv_b),
        "lru_wax": jnp.concatenate([_block_diag_groups(lru_w_a), _block_diag_groups(lru_w_x)], axis=-1).astype(bf16),
        "lru_b_a": row(lru_b_a), "lru_b_x": row(lru_b_x), "lru_lambda": row(lru_lambda),
        "ssd_conv_w": ssd_conv_w, "ssd_conv_b": row(ssd_conv_b),
        "ssd_dt_bias": pad_lanes(row(ssd_dt_bias)),
        "ssd_dt_bias_e": col(ssd_dt_bias[:, 0::2]), "ssd_dt_bias_o": col(ssd_dt_bias[:, 1::2]),
        "ssd_a_log": pad_lanes(row(ssd_a_log)),
        "ssd_a_log_e": col(ssd_a_log[:, 0::2]), "ssd_a_log_o": col(ssd_a_log[:, 1::2]),
        "ssd_d": row(jnp.repeat(ssd_d, SSD_HEAD_DIM, axis=-1)),
        "ssd_norm": row(ssd_norm),
        "ssd_expand3": (jnp.arange(3 * LANES)[:, None] % LANES
                        == (jnp.arange(inner) // SSD_HEAD_DIM)[None, :]).astype(bf16),
        "w_branch": w_branch.astype(bf16), "w_out": w_out.astype(bf16),
    }


def _pad_conv_state(s):
    return jnp.pad(s, ((0, 0), (SUBLANES - s.shape[1], 0), (0, 0)))


def _layer(x, wts, l, slopes, sink, state, tiles, pos0, final_norm):
    B, L, D = x.shape
    cache_k, cache_v, lru_conv, lru_h, ssd_conv, ssd_h = state
    tl, tq_ssd, q, tq_att = tiles
    kw = N_KV_HEADS * HEAD_DIM
    inner = SSD_HEADS * SSD_HEAD_DIM
    ffn_w = (wts["ffn_norm"], wts["ffn_w_gate"], wts["ffn_w_up"], wts["ffn_w_down"])

    x = _ffn(x.reshape(B * L, D), *ffn_w, l, 0).reshape(B, L, D)
    ya, lru_conv8, lru_hn = _lru(x, wts, l, _pad_conv_state(lru_conv), lru_h[:, None, :], tl)
    yb, ssd_conv8, ssd_hn = _ssd(x, wts, l, _pad_conv_state(ssd_conv), ssd_h.reshape(B, inner, SSD_STATE), tq_ssd, q)
    yc, kn, vn = _attn(x, wts, l, slopes, sink, cache_k.reshape(B, WINDOW, kw), cache_v.reshape(B, WINDOW, kw),
                       tq_att, pos0)
    flat = lambda a: a.reshape(B * L, a.shape[-1])
    x = _merge(flat(x), flat(ya), flat(yb), flat(yc), wts, l)
    x = _ffn(x, *ffn_w, l, 1, final_norm=final_norm).reshape(B, L, D)
    new_state = (kn.reshape(B, WINDOW, N_KV_HEADS, HEAD_DIM), vn.reshape(B, WINDOW, N_KV_HEADS, HEAD_DIM),
                 lru_conv8[:, SUBLANES - (CONV_W - 1):], lru_hn[:, 0],
                 ssd_conv8[:, SUBLANES - (CONV_W - 1):], ssd_hn.reshape(B, SSD_HEADS, SSD_HEAD_DIM, SSD_STATE))
    return x, new_state


def kernel(x_prompt, x_sample, cache_attn_k, cache_attn_v, state_lru_conv, state_lru_h, state_ssd_conv, state_ssd_h, ffn_norm, ffn_w_gate, ffn_w_up, ffn_w_down, mix_norm, w_in, lru_conv_w, lru_conv_b, lru_w_a, lru_b_a, lru_w_x, lru_b_x, lru_lambda, ssd_conv_w, ssd_conv_b, ssd_dt_bias, ssd_a_log, ssd_d, ssd_norm, attn_sink, w_branch, w_out, final_norm):
    depth = mix_norm.shape[0]
    bp, lp, D = x_prompt.shape
    bs, ls, _ = x_sample.shape
    wts = _prep_weights(ffn_norm, ffn_w_gate, ffn_w_up, ffn_w_down, mix_norm, w_in, lru_conv_w, lru_conv_b, lru_w_a,
                        lru_b_a, lru_w_x, lru_b_x, lru_lambda, ssd_conv_w, ssd_conv_b, ssd_dt_bias, ssd_a_log, ssd_d,
                        ssd_norm, w_branch, w_out)
    slopes = jnp.exp2(-8.0 * (jnp.arange(N_HEADS, dtype=f32) + 1.0) / N_HEADS)
    fin = final_norm.reshape(1, D)
    kw = N_KV_HEADS * HEAD_DIM
    zero_state = (jnp.zeros((bp, WINDOW, kw), f32), jnp.zeros((bp, WINDOW, kw), f32),
                  jnp.zeros((bp, CONV_W - 1, lru_conv_w.shape[-1]), f32), jnp.zeros((bp, lru_conv_w.shape[-1]), f32),
                  jnp.zeros((bp, CONV_W - 1, ssd_conv_w.shape[-1]), f32),
                  jnp.zeros((bp, SSD_HEADS, SSD_HEAD_DIM, SSD_STATE), f32))
    tiles_p = (min(lp, 512), min(lp, 256), CHUNK, min(lp, 512))
    tiles_s = (ls, ls, ls, ls)
    xp, xs = x_prompt, x_sample
    p_new = [[] for _ in range(6)]
    s_new = [[] for _ in range(6)]
    for l in range(depth):
        last = fin if l == depth - 1 else None
        xp, st_p = _layer(xp, wts, l, slopes, attn_sink[l], zero_state, tiles_p, 0, last)
        st_s = (cache_attn_k[l], cache_attn_v[l], state_lru_conv[l], state_lru_h[l], state_ssd_conv[l], state_ssd_h[l])
        xs, st_sn = _layer(xs, wts, l, slopes, attn_sink[l], st_s, tiles_s, PAST_LEN, last)
        for j in range(6):
            p_new[j].append(st_p[j])
            s_new[j].append(st_sn[j])
    pk, pv, plc, plh, psc, psh = [jnp.stack(t, axis=0) for t in p_new]
    sk, sv, slc, slh, ssc, ssh = [jnp.stack(t, axis=0) for t in s_new]
    return (xp, xs, pk, pv, plc, plh, psc, psh, sk, sv, slc, slh, ssc, ssh)
```

```python
import functools
import math

import jax
import jax.numpy as jnp
from jax import lax
from jax.experimental import pallas as pl
from jax.experimental.pallas import tpu as pltpu

f32 = jnp.float32
bf16 = jnp.bfloat16

EPS = 1e-6
CHUNK = 64
CONV_W = 4
LRU_BLOCKS = 16
LRU_C = 8.0
LRU_GROUP = 256
SSD_HEADS = 16
SSD_HEAD_DIM = 64
SSD_GROUPS = 2
SSD_STATE = 128
SSD_SUBTILE = 256
N_HEADS = 16
N_KV_HEADS = 4
HEAD_DIM = 64
WINDOW = 128
WIN_CHUNKS = WINDOW // CHUNK
PAST_LEN = 4096
NEG = -1e30
MASK_DIST = -NEG * 2.0 ** 8
ATT_SUB = 128
ATT_STACK_ROWS = 64
SEQ_PER_STEP = 1
LANES = 128
SUBLANES = 8
VMEM_LIMIT = 56 * 1024 * 1024

_HI = lax.Precision.HIGHEST


def _rms(x, g):
    return x * lax.rsqrt(jnp.mean(x * x, axis=-1, keepdims=True) + EPS) * g


def _silu(x):
    return x * jax.nn.sigmoid(x)


def _softplus(x):
    return jnp.maximum(x, 0.0) + jnp.log1p(jnp.exp(-jnp.abs(x)))


def _gelu_tanh(x):
    c = math.sqrt(2.0 / math.pi)
    return x * (0.5 * (1.0 + jnp.tanh(c * (x + 0.044715 * (x * x * x)))))


def _dot(a, b):
    return jnp.dot(a, b, preferred_element_type=f32)


def _dot_nt(a, b):
    return lax.dot_general(a, b, (((1,), (1,)), ((), ())), preferred_element_type=f32)


def _dot_tn(a, b):
    return lax.dot_general(a, b, (((0,), (0,)), ((), ())), preferred_element_type=f32)


def _causal_conv(tail_ref, x, cw, cb):
    rows = x.shape[0]
    ext = jnp.concatenate([tail_ref[...], x], axis=0)
    y = cb + cw[CONV_W - 1:CONV_W, :] * x
    for s in range(1, CONV_W):
        y = y + cw[CONV_W - 1 - s:CONV_W - s, :] * pltpu.roll(ext, s, 0)[SUBLANES:SUBLANES + rows]
    tail_ref[...] = x[rows - SUBLANES:, :]
    return y


def _resident(shape, index_map):
    return pl.BlockSpec(shape, index_map, pipeline_mode=pl.Buffered(1))


def _ffn_kernel(*refs, n_chunks, cf, final):
    if final:
        x_ref, g_ref, wg_ref, wu_ref, wd_ref, fg_ref, o_ref, a_ref = refs
    else:
        x_ref, g_ref, wg_ref, wu_ref, wd_ref, o_ref, a_ref = refs
    x = x_ref[...]
    n = _rms(x, g_ref[...]).astype(bf16)
    for c in range(n_chunks):
        sl = slice(c * cf, (c + 1) * cf)
        gate = _dot(n, wg_ref[:, sl])
        up = _dot(n, wu_ref[:, sl])
        a_ref[:, sl] = (_silu(gate) * up).astype(bf16)
    y = x + 0.5 * _dot(a_ref[...], wd_ref[...])
    if final:
        y = _rms(y, fg_ref[...])
    o_ref[...] = y


def _ffn(x, norm, wg, wu, wd, l, j, final_norm=None):
    T, D = x.shape
    F = wg.shape[-1]
    tm = min(T, 1024)
    cf = 256
    final = final_norm is not None
    wmap = lambda i: (l, j, 0, 0)
    in_specs = [
        pl.BlockSpec((tm, D), lambda i: (i, 0)),
        _resident((None, None, 1, D), wmap),
        _resident((None, None, D, F), wmap),
        _resident((None, None, D, F), wmap),
        _resident((None, None, F, D), wmap),
    ]
    args = [x, norm, wg, wu, wd]
    if final:
        in_specs.append(_resident((1, D), lambda i: (0, 0)))
        args.append(final_norm)
    return pl.pallas_call(
        functools.partial(_ffn_kernel, n_chunks=F // cf, cf=cf, final=final),
        grid=(T // tm,),
        in_specs=in_specs,
        out_specs=pl.BlockSpec((tm, D), lambda i: (i, 0)),
        out_shape=jax.ShapeDtypeStruct((T, D), f32),
        scratch_shapes=[pltpu.VMEM((tm, F), bf16)],
        compiler_params=pltpu.CompilerParams(dimension_semantics=("parallel",), vmem_limit_bytes=VMEM_LIMIT),
        name="ffn_final" if final else "ffn",
    )(*args)


def _lru_kernel(x_ref, g_ref, w_ref, cw_ref, cb_ref, wax_ref, ba_ref, bx_ref, lam_ref, conv0_ref, h0_ref,
                y_ref, convo_ref, ho_ref, tail_ref, hc_ref, a_scr, u_scr, gg_scr, *, width):
    W = width
    nb, rows = x_ref.shape[0], x_ref.shape[1]

    @pl.when(pl.program_id(1) == 0)
    def _():
        tail_ref[...] = conv0_ref[...]
        hc_ref[...] = h0_ref[...]

    G = LRU_GROUP
    sub = lax.broadcasted_iota(jnp.int32, (rows, G), 0) & (SUBLANES - 1)
    sp = _softplus(-lam_ref[...])
    for bi in range(nb):
        n = _rms(x_ref[bi], g_ref[...]).astype(bf16)
        for k in range(W // G):
            cs = slice(k * G, (k + 1) * G)
            xl = _dot(n, w_ref[:, cs])
            gg_scr[bi, :, cs] = _gelu_tanh(_dot(n, w_ref[:, W + k * G:W + (k + 1) * G]))
            xa = _causal_conv(tail_ref.at[bi, :, cs], xl, cw_ref[:, cs], cb_ref[:, cs])
            convo_ref[bi, :, cs] = xl[rows - SUBLANES:, :]

            pa = _dot(xa.astype(bf16), wax_ref[k])
            r = jax.nn.sigmoid(pa[:, :G] + ba_ref[:, cs])
            i = jax.nn.sigmoid(pa[:, G:] + bx_ref[:, cs])
            log_a = (-LRU_C) * r * sp[:, cs]
            a = jnp.exp(log_a)
            th = jnp.tanh(log_a)
            u = jnp.sqrt(-2.0 * th / (1.0 - th)) * (i * xa)

            for s in (1, 2, 4):
                keep = sub >= s
                u = jnp.where(keep, a * pltpu.roll(u, s, 0) + u, u)
                a = jnp.where(keep, a * pltpu.roll(a, s, 0), a)
            a_scr[bi, :, cs] = a
            u_scr[bi, :, cs] = u

    def group(k, hs):
        off = pl.multiple_of(k * SUBLANES, SUBLANES)
        out = []
        for bi in range(nb):
            hg = a_scr[bi, pl.ds(off, SUBLANES), :] * hs[bi] + u_scr[bi, pl.ds(off, SUBLANES), :]
            u_scr[bi, pl.ds(off, SUBLANES), :] = hg
            out.append(hg[SUBLANES - 1:SUBLANES, :])
        return tuple(out)

    h_last = lax.fori_loop(0, rows // SUBLANES, group, tuple(hc_ref[bi] for bi in range(nb)))
    for bi in range(nb):
        hc_ref[bi] = h_last[bi]
        ho_ref[bi] = h_last[bi]
        y_ref[bi] = u_scr[bi] * gg_scr[bi]


def _lru_strided_kernel(x_ref, g_ref, w_ref, cw_ref, cb_ref, wax_ref, ba_ref, bx_ref, lam_ref, conv0_ref, h0_ref,
                        y_ref, convo_ref, ho_ref, tail_ref, hc_ref, *, width):
    W, G = width, LRU_GROUP
    R, nblk = x_ref.shape[0], x_ref.shape[1]

    @pl.when(pl.program_id(1) == 0)
    def _():
        tail_ref[...] = conv0_ref[...]
        hc_ref[...] = h0_ref[...]

    rowi = lax.broadcasted_iota(jnp.int32, (R, G), 0)
    row0 = rowi == 0
    sp = _softplus(-lam_ref[...])
    n = _rms(jnp.concatenate([x_ref[:, j, :] for j in range(nblk)], axis=0), g_ref[...]).astype(bf16)
    blk = lambda a, j: a[j * R:(j + 1) * R]

    for k in range(W // G):
        cs = slice(k * G, (k + 1) * G)
        xl = _dot(n, w_ref[:, cs])
        gg = _gelu_tanh(_dot(n, w_ref[:, W + k * G:W + (k + 1) * G]))
        tails = tail_ref[:, cs]
        cw = cw_ref[:, cs]
        prev_group = {j: jnp.where(row0, tails[j:j + 1, :], pltpu.roll(blk(xl, j), 1, 0))
                      for j in range(nblk - CONV_W + 1, nblk)}
        back = lambda j, s: blk(xl, j - s) if j >= s else prev_group[j - s + nblk]
        xa = jnp.concatenate(
            [cb_ref[:, cs] + cw[CONV_W - 1:CONV_W, :] * blk(xl, j)
             + sum(cw[CONV_W - 1 - s:CONV_W - s, :] * back(j, s) for s in range(1, CONV_W))
             for j in range(nblk)], axis=0)
        new_tail = jnp.concatenate([blk(xl, j)[R - 1:R, :] for j in range(nblk)], axis=0)
        tail_ref[:, cs] = new_tail
        convo_ref[:, cs] = new_tail

        pa = _dot(xa.astype(bf16), wax_ref[k])
        r = jax.nn.sigmoid(pa[:, :G] + ba_ref[:, cs])
        i = jax.nn.sigmoid(pa[:, G:] + bx_ref[:, cs])
        log_a = (-LRU_C) * r * sp[:, cs]
        a = jnp.exp(log_a)
        th = jnp.tanh(log_a)
        u = jnp.sqrt(-2.0 * th / (1.0 - th)) * (i * xa)

        acc_a, acc_u = [blk(a, 0)], [blk(u, 0)]
        for j in range(1, nblk):
            acc_u.append(blk(a, j) * acc_u[-1] + blk(u, j))
            acc_a.append(blk(a, j) * acc_a[-1])
        ga, gu = acc_a[-1], acc_u[-1]
        s = 1
        while s < R:
            keep = rowi >= s
            gu = jnp.where(keep, ga * pltpu.roll(gu, s, 0) + gu, gu)
            ga = jnp.where(keep, ga * pltpu.roll(ga, s, 0), ga)
            s *= 2
        h0 = hc_ref[:, cs]
        h_end = ga * h0 + gu
        h_in = jnp.where(row0, h0, pltpu.roll(h_end, 1, 0))
        hc_ref[:, cs] = h_end[R - 1:R, :]
        ho_ref[:, cs] = h_end[R - 1:R, :]
        for j in range(nblk):
            y_ref[:, j, cs] = (acc_a[j] * h_in + acc_u[j]) * blk(gg, j)


def _lru(x, wts, l, conv8, h0, tl):
    B, L, D = x.shape
    W = conv8.shape[-1]
    nb = SEQ_PER_STEP
    lmap = lambda b, t: (l, 0, 0)
    weights = (wts["mix_norm"], wts["w_all"], wts["lru_conv_w"], wts["lru_conv_b"], wts["lru_wax"], wts["lru_b_a"],
               wts["lru_b_x"], wts["lru_lambda"])
    weight_specs = [
        _resident((None, 1, D), lmap),
        _resident((None, D, 2 * W), lmap),
        _resident((None, CONV_W, W), lmap),
        _resident((None, 1, W), lmap),
        _resident((None, W // LRU_GROUP, LRU_GROUP, 2 * LRU_GROUP), lambda b, t: (l, 0, 0, 0)),
        _resident((None, 1, W), lmap),
        _resident((None, 1, W), lmap),
        _resident((None, 1, W), lmap),
    ]
    if tl % (SUBLANES * SUBLANES) == 0:
        r = tl // SUBLANES
        state = lambda rows: pl.BlockSpec((None, rows, W), lambda b, t: (b, 0, 0))
        y, conv_out, h_out = pl.pallas_call(
            functools.partial(_lru_strided_kernel, width=W),
            grid=(B, L // tl),
            in_specs=[pl.BlockSpec((None, r, SUBLANES, D), lambda b, t: (b, t, 0, 0))] + weight_specs
            + [state(SUBLANES), state(1)],
            out_specs=[pl.BlockSpec((None, r, SUBLANES, W), lambda b, t: (b, t, 0, 0)), state(SUBLANES), state(1)],
            out_shape=[
                jax.ShapeDtypeStruct((B, L // SUBLANES, SUBLANES, W), f32),
                jax.ShapeDtypeStruct((B, SUBLANES, W), f32),
                jax.ShapeDtypeStruct((B, 1, W), f32),
            ],
            scratch_shapes=[pltpu.VMEM((SUBLANES, W), f32), pltpu.VMEM((1, W), f32)],
            compiler_params=pltpu.CompilerParams(dimension_semantics=("arbitrary", "arbitrary"),
                                                 vmem_limit_bytes=VMEM_LIMIT),
            name="lru",
        )(x.reshape(B, L // SUBLANES, SUBLANES, D), *weights, conv8, h0)
        return y.reshape(B, L, W), conv_out, h_out
    return pl.pallas_call(
        functools.partial(_lru_kernel, width=W),
        grid=(B // nb, L // tl),
        in_specs=[
            pl.BlockSpec((nb, tl, D), lambda b, t: (b, t, 0)),
            _resident((None, 1, D), lmap),
            _resident((None, D, 2 * W), lmap),
            _resident((None, CONV_W, W), lmap),
            _resident((None, 1, W), lmap),
            _resident((None, W // LRU_GROUP, LRU_GROUP, 2 * LRU_GROUP), lambda b, t: (l, 0, 0, 0)),
            _resident((None, 1, W), lmap),
            _resident((None, 1, W), lmap),
            _resident((None, 1, W), lmap),
            pl.BlockSpec((nb, SUBLANES, W), lambda b, t: (b, 0, 0)),
            pl.BlockSpec((nb, 1, W), lambda b, t: (b, 0, 0)),
        ],
        out_specs=[
            pl.BlockSpec((nb, tl, W), lambda b, t: (b, t, 0)),
            pl.BlockSpec((nb, SUBLANES, W), lambda b, t: (b, 0, 0)),
            pl.BlockSpec((nb, 1, W), lambda b, t: (b, 0, 0)),
        ],
        out_shape=[
            jax.ShapeDtypeStruct((B, L, W), f32),
            jax.ShapeDtypeStruct((B, SUBLANES, W), f32),
            jax.ShapeDtypeStruct((B, 1, W), f32),
        ],
        scratch_shapes=[
            pltpu.VMEM((nb, SUBLANES, W), f32),
            pltpu.VMEM((nb, 1, W), f32),
            pltpu.VMEM((nb, tl, W), f32),
            pltpu.VMEM((nb, tl, W), f32),
            pltpu.VMEM((nb, tl, W), f32),
        ],
        compiler_params=pltpu.CompilerParams(dimension_semantics=("arbitrary", "arbitrary"),
                                             vmem_limit_bytes=VMEM_LIMIT),
        name="lru",
    )(x, wts["mix_norm"], wts["w_all"], wts["lru_conv_w"], wts["lru_conv_b"], wts["lru_wax"], wts["lru_b_a"],
      wts["lru_b_x"], wts["lru_lambda"], conv8, h0)


def _split3(x):
    a = x.astype(bf16)
    r = x - a.astype(f32)
    b = r.astype(bf16)
    return a, b, (r - b.astype(f32)).astype(bf16)


def _ssd_kernel(x_ref, g_ref, wz_ref, wx_ref, wdt_ref, wdte_ref, wdto_ref, cw_ref, cb_ref, dtb_ref, dtbe_ref, dtbo_ref,
                alog_ref, aloge_ref, alogo_ref, dskip_ref, ng_ref, e3_ref, conv0_ref, h0_ref,
                y_ref, convo_ref, ho_ref,
                tail_ref, hst_ref, y_scr, st_scr, *, q):
    @pl.when(pl.program_id(1) == 0)
    def _():
        tail_ref[...] = conv0_ref[...]
        hst_ref[...] = h0_ref[...]

    rows = x_ref.shape[1]
    sub = min(rows, SSD_SUBTILE)
    for bi in range(x_ref.shape[0]):
        for r0 in range(0, rows, sub):
            rs = pl.ds(r0, sub)
            _ssd_tile(x_ref.at[bi, rs], g_ref, wz_ref, wx_ref, wdt_ref, wdte_ref, wdto_ref, cw_ref, cb_ref, dtb_ref,
                      dtbe_ref, dtbo_ref, alog_ref, aloge_ref, alogo_ref, dskip_ref, ng_ref, e3_ref,
                      y_ref.at[bi, rs], convo_ref.at[bi], ho_ref.at[bi],
                      tail_ref.at[bi], hst_ref.at[bi], y_scr.at[bi], st_scr.at[bi], q=q)


def _ssd_tile(x_ref, g_ref, wz_ref, wx_ref, wdt_ref, wdte_ref, wdto_ref, cw_ref, cb_ref, dtb_ref, dtbe_ref, dtbo_ref,
              alog_ref, aloge_ref, alogo_ref, dskip_ref, ng_ref, e3_ref,
              y_ref, convo_ref, ho_ref, tail_ref, hst_ref, y_scr, st_scr, *, q):
    rows = x_ref.shape[0]
    nch = rows // q
    inner = SSD_HEADS * SSD_HEAD_DIM
    gn = SSD_GROUPS * SSD_STATE
    ppg = SSD_HEADS // 2 // SSD_GROUPS
    gw = inner // SSD_GROUPS

    n = _rms(x_ref[...], g_ref[...]).astype(bf16)

    def conv_cols(cols):
        xr = _dot(n, wx_ref[:, cols])
        convo_ref[:, cols] = xr[rows - SUBLANES:, :]
        return _silu(_causal_conv(tail_ref.at[:, cols], xr, cw_ref[:, cols], cb_ref[:, cols]))

    ri = lax.broadcasted_iota(jnp.int32, (rows, rows), 0)
    ci = lax.broadcasted_iota(jnp.int32, (rows, rows), 1)
    same_blk = (ri // q) == (ci // q)
    tri_bd = (same_blk & (ci <= ri)).astype(f32)
    tri_bd_t = (same_blk & (ri <= ci)).astype(f32)
    li = lax.broadcasted_iota(jnp.int32, (q, 2 * q), 0)
    lj = lax.broadcasted_iota(jnp.int32, (q, 2 * q), 1)
    causal2 = jnp.where(lj >= q, lj - q, lj) <= li
    lane_hi = lax.broadcasted_iota(jnp.int32, (q, LANES), 1) >= SSD_HEAD_DIM
    a_neg = -jnp.exp(alog_ref[...])
    a_neg_e = -jnp.exp(aloge_ref[...])
    a_neg_o = -jnp.exp(alogo_ref[...])

    dt = _softplus(_dot(n, wdt_ref[...]) + dtb_ref[...])
    cum = jnp.dot(tri_bd, dt * a_neg, precision=_HI, preferred_element_type=f32)
    last = jnp.concatenate([jnp.broadcast_to(cum[(c + 1) * q - 1:(c + 1) * q, :], (q, LANES)) for c in range(nch)],
                           axis=0)
    decay_end = jnp.exp(last - cum) * dt
    cum3 = jnp.concatenate(_split3(cum), axis=1)
    dec3 = jnp.concatenate(_split3(decay_end), axis=1)
    dt_e = _softplus(_dot_nt(wdte_ref[...], n) + dtbe_ref[...])
    dt_o = _softplus(_dot_nt(wdto_ref[...], n) + dtbo_ref[...])
    cum_e = jnp.dot(dt_e * a_neg_e, tri_bd_t, precision=_HI, preferred_element_type=f32)
    cum_o = jnp.dot(dt_o * a_neg_o, tri_bd_t, precision=_HI, preferred_element_type=f32)
    chunks = [slice(c * q, (c + 1) * q) for c in range(nch)]
    dt_t2 = [jnp.concatenate([dt_e[:, rc], dt_o[:, rc]], axis=1) for rc in chunks]
    cum_t2 = [jnp.concatenate([cum_e[:, rc], cum_o[:, rc]], axis=1) for rc in chunks]

    for g in range(SSD_GROUPS):
        gs = slice(g * gw, (g + 1) * gw)
        xs = conv_cols(gs)
        bgb = conv_cols(slice(inner + g * SSD_STATE, inner + (g + 1) * SSD_STATE)).astype(bf16)
        cgb = conv_cols(slice(inner + gn + g * SSD_STATE, inner + gn + (g + 1) * SSD_STATE)).astype(bf16)
        cum_x = _dot(cum3, e3_ref[:, gs])
        e_x = jnp.exp(cum_x)
        xd = (xs * _dot(dec3, e3_ref[:, gs])).astype(bf16)
        y_scr[:, gs] = dskip_ref[:, gs] * xs

        for c, rc in enumerate(chunks):
            cb2 = _dot_nt(cgb[rc], jnp.concatenate([bgb[rc], bgb[rc]], axis=0))
            for pr in range(ppg):
                m = g * ppg + pr
                ps = slice(pr * LANES, (pr + 1) * LANES)
                cx = cum_x[rc, ps]
                if 2 * q != LANES:
                    cx = jnp.concatenate([cx[:, :q], cx[:, SSD_HEAD_DIM:SSD_HEAD_DIM + q]], axis=1)
                wm = cb2 * jnp.where(causal2, jnp.exp(cx - cum_t2[c][m:m + 1, :]), 0.0) * dt_t2[c][m:m + 1, :]
                xp = xs[rc, ps]
                x_bd = jnp.concatenate([jnp.where(lane_hi, 0.0, xp), jnp.where(lane_hi, xp, 0.0)], axis=0)
                y_scr[rc, m * LANES:(m + 1) * LANES] += _dot(wm.astype(bf16), x_bd.astype(bf16))
            st_scr[c, :, gs] = _dot_tn(bgb[rc], xd[rc])

        for c, rc in enumerate(chunks):
            y_off = _dot(cgb[rc], hst_ref[:, gs].astype(bf16))
            y_scr[rc, gs] += y_off * e_x[rc]
            hst_ref[:, gs] = e_x[(c + 1) * q - 1:(c + 1) * q, :] * hst_ref[:, gs] + st_scr[c, :, gs]

        y = y_scr[:, gs] * _silu(_dot(n, wz_ref[:, gs]))
        y_ref[:, gs] = _rms(y, ng_ref[:, gs])
    ho_ref[...] = hst_ref[...]


def _ssd(x, wts, l, conv8, h0, tq, q):
    B, L, D = x.shape
    inner = SSD_HEADS * SSD_HEAD_DIM
    cdim = inner + 2 * SSD_GROUPS * SSD_STATE
    half = SSD_HEADS // 2
    nb = SEQ_PER_STEP
    lmap = lambda b, t: (l, 0, 0)
    return pl.pallas_call(
        functools.partial(_ssd_kernel, q=q),
        grid=(B // nb, L // tq),
        in_specs=[
            pl.BlockSpec((nb, tq, D), lambda b, t: (b, t, 0)),
            _resident((None, 1, D), lmap),
            _resident((None, D, inner), lambda b, t: (l, 0, wts["ssd_z_col"] // inner)),
            _resident((None, D, cdim), lambda b, t: (l, 0, wts["ssd_x_col"] // cdim)),
            _resident((None, D, LANES), lmap),
            _resident((None, half, D), lmap),
            _resident((None, half, D), lmap),
            _resident((None, CONV_W, cdim), lmap),
            _resident((None, 1, cdim), lmap),
            _resident((None, 1, LANES), lmap),
            _resident((None, half, 1), lmap),
            _resident((None, half, 1), lmap),
            _resident((None, 1, LANES), lmap),
            _resident((None, half, 1), lmap),
            _resident((None, half, 1), lmap),
            _resident((None, 1, inner), lmap),
            _resident((None, 1, inner), lmap),
            _resident((3 * LANES, inner), lambda b, t: (0, 0)),
            pl.BlockSpec((nb, SUBLANES, cdim), lambda b, t: (b, 0, 0)),
            pl.BlockSpec((nb, SSD_STATE, inner), lambda b, t: (b, 0, 0)),
        ],
        out_specs=[
            pl.BlockSpec((nb, tq, inner), lambda b, t: (b, t, 0)),
            pl.BlockSpec((nb, SUBLANES, cdim), lambda b, t: (b, 0, 0)),
            pl.BlockSpec((nb, SSD_STATE, inner), lambda b, t: (b, 0, 0)),
        ],
        out_shape=[
            jax.ShapeDtypeStruct((B, L, inner), f32),
            jax.ShapeDtypeStruct((B, SUBLANES, cdim), f32),
            jax.ShapeDtypeStruct((B, SSD_STATE, inner), f32),
        ],
        scratch_shapes=[
            pltpu.VMEM((nb, SUBLANES, cdim), f32),
            pltpu.VMEM((nb, SSD_STATE, inner), f32),
            pltpu.VMEM((nb, min(tq, SSD_SUBTILE), inner), f32),
            pltpu.VMEM((nb, min(tq, SSD_SUBTILE) // q, SSD_STATE, inner), f32),
        ],
        compiler_params=pltpu.CompilerParams(dimension_semantics=("arbitrary", "arbitrary"),
                                             vmem_limit_bytes=VMEM_LIMIT),
        name="ssd",
    )(x, wts["mix_norm"], wts["w_all"], wts["w_all"], wts["w_dt"], wts["w_dt_e"], wts["w_dt_o"], wts["ssd_conv_w"],
      wts["ssd_conv_b"], wts["ssd_dt_bias"], wts["ssd_dt_bias_e"], wts["ssd_dt_bias_o"], wts["ssd_a_log"],
      wts["ssd_a_log_e"], wts["ssd_a_log_o"], wts["ssd_d"], wts["ssd_norm"], wts["ssd_expand3"], conv8, h0)


def _attn_kernel(slope_ref, sink_ref, x_ref, g_ref, w_ref, pk_ref, pv_ref,
                 y_ref, ko_ref, vo_ref, kp_scr, vp_scr, *, pos0, sub):
    @pl.when(pl.program_id(1) == 0)
    def _():
        kp_scr[...] = pk_ref[...]
        vp_scr[...] = pv_ref[...]

    for bi in range(x_ref.shape[0]):
        _attn_tile(slope_ref, sink_ref, x_ref.at[bi], g_ref, w_ref, y_ref.at[bi], ko_ref.at[bi], vo_ref.at[bi],
                   kp_scr.at[bi], vp_scr.at[bi], pos0=pos0, sub=sub)


def _attn_tile(slope_ref, sink_ref, x_ref, g_ref, w_ref, y_ref, ko_ref, vo_ref, kp_scr, vp_scr, *, pos0, sub):
    rows = x_ref.shape[0]
    s_len = WINDOW + sub
    aw = N_HEADS * HEAD_DIM
    kw = N_KV_HEADS * HEAD_DIM
    gqa = N_HEADS // N_KV_HEADS
    t = pl.program_id(1)

    n = _rms(x_ref[...], g_ref[...]).astype(bf16)
    p = _dot(n, w_ref[...])
    qq = p[:, :aw] * (HEAD_DIM ** -0.5)
    kk = jnp.concatenate([kp_scr[...], p[:, aw:aw + kw]], axis=0)
    vv = jnp.concatenate([vp_scr[...], p[:, aw + kw:aw + 2 * kw]], axis=0)
    k_new = kk[rows:, :]
    v_new = vv[rows:, :]
    kp_scr[...] = k_new
    vp_scr[...] = v_new
    ko_ref[...] = k_new
    vo_ref[...] = v_new

    iq = lax.broadcasted_iota(jnp.int32, (sub, s_len), 0)
    ik = lax.broadcasted_iota(jnp.int32, (sub, s_len), 1)
    neg_dist = -jnp.abs(iq - (ik - WINDOW)).astype(f32)
    qc = iq // CHUNK
    kc = ik // CHUNK - WIN_CHUNKS
    in_band = (kc >= qc - WIN_CHUNKS) & (kc <= qc)
    lane_q = lax.broadcasted_iota(jnp.int32, (sub, LANES), 1) >= HEAD_DIM
    lane_k = lax.broadcasted_iota(jnp.int32, (WINDOW + rows, LANES), 1) >= HEAD_DIM

    k_both, v_lo, v_hi = [], [], []
    for kvh in range(N_KV_HEADS):
        tile = slice((kvh // 2) * LANES, (kvh // 2 + 1) * LANES)
        odd = kvh % 2 == 1
        k_own = jnp.where(lane_k, kk[:, tile], 0.0) if odd else jnp.where(lane_k, 0.0, kk[:, tile])
        v_own = jnp.where(lane_k, vv[:, tile], 0.0) if odd else jnp.where(lane_k, 0.0, vv[:, tile])
        v_other = pltpu.roll(v_own, HEAD_DIM, 1)
        k_both.append((k_own + pltpu.roll(k_own, HEAD_DIM, 1)).astype(bf16))
        v_lo.append((v_other if odd else v_own).astype(bf16))
        v_hi.append((v_own if odd else v_other).astype(bf16))

    for st in range(rows // sub):
        r0 = st * sub
        k_pos = ik + (pos0 - WINDOW + r0 + t * rows)
        bias_base = jnp.where(in_band & (k_pos >= 0), neg_dist, -MASK_DIST)
        for kvh in range(N_KV_HEADS):
            if gqa * sub > ATT_STACK_ROWS:
                for m in range(kvh * gqa // 2, (kvh + 1) * gqa // 2):
                    qp = qq[r0:r0 + sub, m * LANES:(m + 1) * LANES]
                    o_pair = jnp.zeros((sub, LANES), f32)
                    for half in range(2):
                        h = 2 * m + half
                        qh = (jnp.where(lane_q, qp, 0.0) if half else jnp.where(lane_q, 0.0, qp)).astype(bf16)
                        s = _dot_nt(qh, k_both[kvh][r0:r0 + s_len]) + slope_ref[h] * bias_base
                        mx = jnp.maximum(jnp.max(s, axis=-1, keepdims=True), sink_ref[h])
                        e = jnp.exp(s - mx)
                        den = jnp.sum(e, axis=-1, keepdims=True) + jnp.exp(sink_ref[h] - mx)
                        vh = (v_hi if half else v_lo)[kvh][r0:r0 + s_len]
                        o_pair = o_pair + _dot(e.astype(bf16), vh) * (1.0 / den)
                    y_ref[r0:r0 + sub, m * LANES:(m + 1) * LANES] = o_pair
                continue
            q_rows, bias_rows, sink_rows = [], [], []
            for j in range(gqa):
                h = kvh * gqa + j
                qp = qq[r0:r0 + sub, (h // 2) * LANES:(h // 2 + 1) * LANES]
                q_rows.append((jnp.where(lane_q, qp, 0.0) if h % 2 else jnp.where(lane_q, 0.0, qp)).astype(bf16))
                bias_rows.append(slope_ref[h] * bias_base)
                sink_rows.append(jnp.full((sub, 1), sink_ref[h], f32))
            sink_col = jnp.concatenate(sink_rows, axis=0)
            s = (_dot_nt(jnp.concatenate(q_rows, axis=0), k_both[kvh][r0:r0 + s_len])
                 + jnp.concatenate(bias_rows, axis=0))
            mx = jnp.maximum(jnp.max(s, axis=-1, keepdims=True), sink_col)
            e = jnp.exp(s - mx)
            inv = 1.0 / (jnp.sum(e, axis=-1, keepdims=True) + jnp.exp(sink_col - mx))
            eb = e.astype(bf16)
            blk = lambda a, j: a[j * sub:(j + 1) * sub]
            even = [j for j in range(gqa) if (kvh * gqa + j) % 2 == 0]
            odd = [j for j in range(gqa) if (kvh * gqa + j) % 2 == 1]
            o_lo = _dot(jnp.concatenate([blk(eb, j) for j in even], axis=0), v_lo[kvh][r0:r0 + s_len])
            o_hi = _dot(jnp.concatenate([blk(eb, j) for j in odd], axis=0), v_hi[kvh][r0:r0 + s_len])
            for i, (je, jo) in enumerate(zip(even, odd)):
                m = (kvh * gqa + je) // 2
                y_ref[r0:r0 + sub, m * LANES:(m + 1) * LANES] = (blk(o_lo, i) * blk(inv, je)
                                                                 + blk(o_hi, i) * blk(inv, jo))


def _attn(x, wts, l, slopes, sink, prev_k, prev_v, tq, pos0):
    B, L, D = x.shape
    aw = N_HEADS * HEAD_DIM
    kw = N_KV_HEADS * HEAD_DIM
    nb = SEQ_PER_STEP
    lmap = lambda b, t: (l, 0, 0)
    smem = pl.BlockSpec(memory_space=pltpu.SMEM)
    return pl.pallas_call(
        functools.partial(_attn_kernel, pos0=pos0, sub=min(tq, ATT_SUB)),
        grid=(B // nb, L // tq),
        in_specs=[
            smem, smem,
            pl.BlockSpec((nb, tq, D), lambda b, t: (b, t, 0)),
            _resident((None, 1, D), lmap),
            _resident((None, D, aw + 2 * kw), lmap),
            pl.BlockSpec((nb, WINDOW, kw), lambda b, t: (b, 0, 0)),
            pl.BlockSpec((nb, WINDOW, kw), lambda b, t: (b, 0, 0)),
        ],
        out_specs=[
            pl.BlockSpec((nb, tq, aw), lambda b, t: (b, t, 0)),
            pl.BlockSpec((nb, WINDOW, kw), lambda b, t: (b, 0, 0)),
            pl.BlockSpec((nb, WINDOW, kw), lambda b, t: (b, 0, 0)),
        ],
        out_shape=[
            jax.ShapeDtypeStruct((B, L, aw), f32),
            jax.ShapeDtypeStruct((B, WINDOW, kw), f32),
            jax.ShapeDtypeStruct((B, WINDOW, kw), f32),
        ],
        scratch_shapes=[pltpu.VMEM((nb, WINDOW, kw), f32), pltpu.VMEM((nb, WINDOW, kw), f32)],
        compiler_params=pltpu.CompilerParams(dimension_semantics=("arbitrary", "arbitrary"),
                                             vmem_limit_bytes=VMEM_LIMIT),
        name="attn",
    )(slopes, sink, x, wts["mix_norm"], wts["w_qkv"], prev_k, prev_v)


def _merge_kernel(x_ref, ya_ref, yb_ref, yc_ref, g_ref, wg_ref, wb_ref, wo_ref, o_ref):
    x = x_ref[...]
    D = x.shape[-1]
    n = _rms(x, g_ref[...]).astype(bf16)
    merged = jnp.zeros(x.shape, f32)
    for b, y_ref in enumerate((ya_ref, yb_ref, yc_ref)):
        bw = y_ref.shape[-1]
        gate = jax.nn.sigmoid(_dot(n, wg_ref[:, b * D:(b + 1) * D]))
        merged = merged + gate * _dot(y_ref[...].astype(bf16), wb_ref[b * bw:(b + 1) * bw, :])
    o_ref[...] = x + _dot(merged.astype(bf16), wo_ref[...])


def _merge(x, ya, yb, yc, wts, l):
    T, D = x.shape
    tm = min(T, 512)
    lmap = lambda i: (l, 0, 0)
    tok = lambda w: pl.BlockSpec((tm, w), lambda i: (i, 0))
    return pl.pallas_call(
        _merge_kernel,
        grid=(T // tm,),
        in_specs=[
            tok(D), tok(ya.shape[-1]), tok(yb.shape[-1]), tok(yc.shape[-1]),
            _resident((None, 1, D), lmap),
            _resident((None, D, 3 * D), lmap),
            _resident((None, wts["w_branch"].shape[1], D), lmap),
            _resident((None, D, D), lmap),
        ],
        out_specs=tok(D),
        out_shape=jax.ShapeDtypeStruct((T, D), f32),
        compiler_params=pltpu.CompilerParams(dimension_semantics=("parallel",), vmem_limit_bytes=VMEM_LIMIT),
        name="merge",
    )(x, ya, yb, yc, wts["mix_norm"], wts["w_gates"], wts["w_branch"], wts["w_out"])


def _block_diag_groups(w):
    nl, nb, bs, _ = w.shape
    per = LRU_GROUP // bs
    w = w.reshape(nl, nb // per, per, bs, bs)
    eye = jnp.eye(per, dtype=w.dtype)
    out = w[:, :, :, :, None, :] * eye[None, None, :, None, :, None]
    return out.reshape(nl, nb // per, LRU_GROUP, LRU_GROUP)


def _prep_weights(ffn_norm, ffn_w_gate, ffn_w_up, ffn_w_down, mix_norm, w_in, lru_conv_w, lru_conv_b, lru_w_a,
                  lru_b_a, lru_w_x, lru_b_x, lru_lambda, ssd_conv_w, ssd_conv_b, ssd_dt_bias, ssd_a_log, ssd_d,
                  ssd_norm, w_branch, w_out):
    nl, D = mix_norm.shape
    W = lru_conv_w.shape[-1]
    inner = SSD_HEADS * SSD_HEAD_DIM
    cdim = ssd_conv_w.shape[-1]
    aw = N_HEADS * HEAD_DIM
    kw = N_KV_HEADS * HEAD_DIM
    o = [0]
    for s in (W, W, inner, cdim, SSD_HEADS, aw, kw, kw, 3 * D):
        o.append(o[-1] + s)
    row = lambda a: a.reshape(nl, 1, -1)
    pad_lanes = lambda a: jnp.pad(a, [(0, 0)] * (a.ndim - 1) + [(0, LANES - a.shape[-1])])
    assert o[0] == 0 and o[2] % inner == 0 and o[3] % cdim == 0, "column ranges must be whole blocks"
    w_all = w_in.astype(bf16)
    w_dt = w_in[:, :, o[4]:o[5]]
    w_dt_t = jnp.swapaxes(w_dt, 1, 2).astype(bf16)
    col = lambda a: a[:, :, None]
    return {
        "ffn_norm": ffn_norm.reshape(nl, 2, 1, D),
        "ffn_w_gate": ffn_w_gate.astype(bf16), "ffn_w_up": ffn_w_up.astype(bf16), "ffn_w_down": ffn_w_down.astype(bf16),
        "mix_norm": row(mix_norm),
        "w_all": w_all, "ssd_z_col": o[2], "ssd_x_col": o[3],
        "w_dt": pad_lanes(w_dt).astype(bf16),
        "w_dt_e": w_dt_t[:, 0::2], "w_dt_o": w_dt_t[:, 1::2],
        "w_qkv": w_all[:, :, o[5]:o[8]],
        "w_gates": w_all[:, :, o[8]:o[9]],
        "lru_conv_w": lru_conv_w, "lru_conv_b": row(lru_conv_b),
        "lru_wax": jnp.concatenate([_block_diag_groups(lru_w_a), _block_diag_groups(lru_w_x)], axis=-1).astype(bf16),
        "lru_b_a": row(lru_b_a), "lru_b_x": row(lru_b_x), "lru_lambda": row(lru_lambda),
        "ssd_conv_w": ssd_conv_w, "ssd_conv_b": row(ssd_conv_b),
        "ssd_dt_bias": pad_lanes(row(ssd_dt_bias)),
        "ssd_dt_bias_e": col(ssd_dt_bias[:, 0::2]), "ssd_dt_bias_o": col(ssd_dt_bias[:, 1::2]),
        "ssd_a_log": pad_lanes(row(ssd_a_log)),
        "ssd_a_log_e": col(ssd_a_log[:, 0::2]), "ssd_a_log_o": col(ssd_a_log[:, 1::2]),
        "ssd_d": row(jnp.repeat(ssd_d, SSD_HEAD_DIM, axis=-1)),
        "ssd_norm": row(ssd_norm),
        "ssd_expand3": (jnp.arange(3 * LANES)[:, None] % LANES
                        == (jnp.arange(inner) // SSD_HEAD_DIM)[None, :]).astype(bf16),
        "w_branch": w_branch.astype(bf16), "w_out": w_out.astype(bf16),
    }


def _pad_conv_state(s):
    return jnp.pad(s, ((0, 0), (SUBLANES - s.shape[1], 0), (0, 0)))


def _layer(x, wts, l, slopes, sink, state, tiles, pos0, final_norm):
    B, L, D = x.shape
    cache_k, cache_v, lru_conv, lru_h, ssd_conv, ssd_h = state
    tl, tq_ssd, q, tq_att = tiles
    kw = N_KV_HEADS * HEAD_DIM
    inner = SSD_HEADS * SSD_HEAD_DIM
    ffn_w = (wts["ffn_norm"], wts["ffn_w_gate"], wts["ffn_w_up"], wts["ffn_w_down"])

    x = _ffn(x.reshape(B * L, D), *ffn_w, l, 0).reshape(B, L, D)
    ya, lru_conv8, lru_hn = _lru(x, wts, l, _pad_conv_state(lru_conv), lru_h[:, None, :], tl)
    ssd_ht = jnp.swapaxes(ssd_h.reshape(B, inner, SSD_STATE), 1, 2)
    yb, ssd_conv8, ssd_htn = _ssd(x, wts, l, _pad_conv_state(ssd_conv), ssd_ht, tq_ssd, q)
    ssd_hn = jnp.swapaxes(ssd_htn, 1, 2)
    yc, kn, vn = _attn(x, wts, l, slopes, sink, cache_k.reshape(B, WINDOW, kw), cache_v.reshape(B, WINDOW, kw),
                       tq_att, pos0)
    flat = lambda a: a.reshape(B * L, a.shape[-1])
    x = _merge(flat(x), flat(ya), flat(yb), flat(yc), wts, l)
    x = _ffn(x, *ffn_w, l, 1, final_norm=final_norm).reshape(B, L, D)
    new_state = (kn.reshape(B, WINDOW, N_KV_HEADS, HEAD_DIM), vn.reshape(B, WINDOW, N_KV_HEADS, HEAD_DIM),
                 lru_conv8[:, SUBLANES - (CONV_W - 1):], lru_hn[:, 0],
                 ssd_conv8[:, SUBLANES - (CONV_W - 1):], ssd_hn.reshape(B, SSD_HEADS, SSD_HEAD_DIM, SSD_STATE))
    return x, new_state


def kernel(x_prompt, x_sample, cache_attn_k, cache_attn_v, state_lru_conv, state_lru_h, state_ssd_conv, state_ssd_h, ffn_norm, ffn_w_gate, ffn_w_up, ffn_w_down, mix_norm, w_in, lru_conv_w, lru_conv_b, lru_w_a, lru_b_a, lru_w_x, lru_b_x, lru_lambda, ssd_conv_w, ssd_conv_b, ssd_dt_bias, ssd_a_log, ssd_d, ssd_norm, attn_sink, w_branch, w_out, final_norm):
    depth = mix_norm.shape[0]
    bp, lp, D = x_prompt.shape
    bs, ls, _ = x_sample.shape
    wts = _prep_weights(ffn_norm, ffn_w_gate, ffn_w_up, ffn_w_down, mix_norm, w_in, lru_conv_w, lru_conv_b, lru_w_a,
                        lru_b_a, lru_w_x, lru_b_x, lru_lambda, ssd_conv_w, ssd_conv_b, ssd_dt_bias, ssd_a_log, ssd_d,
                        ssd_norm, w_branch, w_out)
    slopes = jnp.exp2(-8.0 * (jnp.arange(N_HEADS, dtype=f32) + 1.0) / N_HEADS)
    fin = final_norm.reshape(1, D)
    kw = N_KV_HEADS * HEAD_DIM
    zero_state = (jnp.zeros((bp, WINDOW, kw), f32), jnp.zeros((bp, WINDOW, kw), f32),
                  jnp.zeros((bp, CONV_W - 1, lru_conv_w.shape[-1]), f32), jnp.zeros((bp, lru_conv_w.shape[-1]), f32),
                  jnp.zeros((bp, CONV_W - 1, ssd_conv_w.shape[-1]), f32),
                  jnp.zeros((bp, SSD_HEADS, SSD_HEAD_DIM, SSD_STATE), f32))
    tiles_p = (min(lp, 512), min(lp, 512), CHUNK, min(lp, 512))
    tiles_s = (ls, ls, ls, ls)
    xp, xs = x_prompt, x_sample
    p_new = [[] for _ in range(6)]
    s_new = [[] for _ in range(6)]
    for l in range(depth):
        last = fin if l == depth - 1 else None
        xp, st_p = _layer(xp, wts, l, slopes, attn_sink[l], zero_state, tiles_p, 0, last)
        st_s = (cache_attn_k[l], cache_attn_v[l], state_lru_conv[l], state_lru_h[l], state_ssd_conv[l], state_ssd_h[l])
        xs, st_sn = _layer(xs, wts, l, slopes, attn_sink[l], st_s, tiles_s, PAST_LEN, last)
        for j in range(6):
            p_new[j].append(st_p[j])
            s_new[j].append(st_sn[j])
    pk, pv, plc, plh, psc, psh = [jnp.stack(t, axis=0) for t in p_new]
    sk, sv, slc, slh, ssc, ssh = [jnp.stack(t, axis=0) for t in s_new]
    return (xp, xs, pk, pv, plc, plh, psc, psh, sk, sv, slc, slh, ssc, ssh)
```

```python
import functools
import math

import jax
import jax.numpy as jnp
from jax import lax
from jax.experimental import pallas as pl
from jax.experimental.pallas import tpu as pltpu

f32 = jnp.float32
bf16 = jnp.bfloat16

EPS = 1e-6
CHUNK = 64
CONV_W = 4
LRU_BLOCKS = 16
LRU_C = 8.0
LRU_GROUP = 256
SSD_HEADS = 16
SSD_HEAD_DIM = 64
SSD_GROUPS = 2
SSD_STATE = 128
SSD_SUBTILE = 256
N_HEADS = 16
N_KV_HEADS = 4
HEAD_DIM = 64
WINDOW = 128
WIN_CHUNKS = WINDOW // CHUNK
PAST_LEN = 4096
NEG = -1e30
MASK_DIST = -NEG * 2.0 ** 8
ATT_SUB = 128
ATT_STACK_ROWS = 64
SEQ_PER_STEP = 1
LANES = 128
SUBLANES = 8
VMEM_LIMIT = 56 * 1024 * 1024

_HI = lax.Precision.HIGHEST


def _rms(x, g):
    return x * lax.rsqrt(jnp.mean(x * x, axis=-1, keepdims=True) + EPS) * g


def _silu(x):
    return x * jax.nn.sigmoid(x)


def _softplus(x):
    return jnp.maximum(x, 0.0) + jnp.log1p(jnp.exp(-jnp.abs(x)))


def _gelu_tanh(x):
    c = math.sqrt(2.0 / math.pi)
    return x * (0.5 * (1.0 + jnp.tanh(c * (x + 0.044715 * (x * x * x)))))


def _dot(a, b):
    return jnp.dot(a, b, preferred_element_type=f32)


def _dot_nt(a, b):
    return lax.dot_general(a, b, (((1,), (1,)), ((), ())), preferred_element_type=f32)


def _dot_tn(a, b):
    return lax.dot_general(a, b, (((0,), (0,)), ((), ())), preferred_element_type=f32)


def _causal_conv(tail_ref, x, cw, cb):
    rows = x.shape[0]
    ext = jnp.concatenate([tail_ref[...], x], axis=0)
    y = cb + cw[CONV_W - 1:CONV_W, :] * x
    for s in range(1, CONV_W):
        y = y + cw[CONV_W - 1 - s:CONV_W - s, :] * pltpu.roll(ext, s, 0)[SUBLANES:SUBLANES + rows]
    tail_ref[...] = x[rows - SUBLANES:, :]
    return y


def _resident(shape, index_map):
    return pl.BlockSpec(shape, index_map, pipeline_mode=pl.Buffered(1))


def _ffn_kernel(*refs, n_chunks, cf, final):
    if final:
        x_ref, g_ref, wg_ref, wu_ref, wd_ref, fg_ref, o_ref, a_ref = refs
    else:
        x_ref, g_ref, wg_ref, wu_ref, wd_ref, o_ref, a_ref = refs
    x = x_ref[...]
    n = _rms(x, g_ref[...]).astype(bf16)
    for c in range(n_chunks):
        sl = slice(c * cf, (c + 1) * cf)
        gate = _dot(n, wg_ref[:, sl])
        up = _dot(n, wu_ref[:, sl])
        a_ref[:, sl] = (_silu(gate) * up).astype(bf16)
    y = x + 0.5 * _dot(a_ref[...], wd_ref[...])
    if final:
        y = _rms(y, fg_ref[...])
    o_ref[...] = y


def _ffn(x, norm, wg, wu, wd, l, j, final_norm=None):
    T, D = x.shape
    F = wg.shape[-1]
    tm = min(T, 1024)
    cf = 256
    final = final_norm is not None
    wmap = lambda i: (l, j, 0, 0)
    in_specs = [
        pl.BlockSpec((tm, D), lambda i: (i, 0)),
        _resident((None, None, 1, D), wmap),
        _resident((None, None, D, F), wmap),
        _resident((None, None, D, F), wmap),
        _resident((None, None, F, D), wmap),
    ]
    args = [x, norm, wg, wu, wd]
    if final:
        in_specs.append(_resident((1, D), lambda i: (0, 0)))
        args.append(final_norm)
    return pl.pallas_call(
        functools.partial(_ffn_kernel, n_chunks=F // cf, cf=cf, final=final),
        grid=(T // tm,),
        in_specs=in_specs,
        out_specs=pl.BlockSpec((tm, D), lambda i: (i, 0)),
        out_shape=jax.ShapeDtypeStruct((T, D), f32),
        scratch_shapes=[pltpu.VMEM((tm, F), bf16)],
        compiler_params=pltpu.CompilerParams(dimension_semantics=("parallel",), vmem_limit_bytes=VMEM_LIMIT),
        name="ffn_final" if final else "ffn",
    )(*args)


def _lru_kernel(x_ref, g_ref, w_ref, cw_ref, cb_ref, wax_ref, ba_ref, bx_ref, lam_ref, conv0_ref, h0_ref,
                y_ref, convo_ref, ho_ref, tail_ref, hc_ref, a_scr, u_scr, gg_scr, *, width):
    W = width
    nb, rows = x_ref.shape[0], x_ref.shape[1]

    @pl.when(pl.program_id(1) == 0)
    def _():
        tail_ref[...] = conv0_ref[...]
        hc_ref[...] = h0_ref[...]

    G = LRU_GROUP
    sub = lax.broadcasted_iota(jnp.int32, (rows, G), 0) & (SUBLANES - 1)
    sp = _softplus(-lam_ref[...])
    for bi in range(nb):
        n = _rms(x_ref[bi], g_ref[...]).astype(bf16)
        for k in range(W // G):
            cs = slice(k * G, (k + 1) * G)
            xl = _dot(n, w_ref[:, cs])
            gg_scr[bi, :, cs] = _gelu_tanh(_dot(n, w_ref[:, W + k * G:W + (k + 1) * G]))
            xa = _causal_conv(tail_ref.at[bi, :, cs], xl, cw_ref[:, cs], cb_ref[:, cs])
            convo_ref[bi, :, cs] = xl[rows - SUBLANES:, :]

            pa = _dot(xa.astype(bf16), wax_ref[k])
            r = jax.nn.sigmoid(pa[:, :G] + ba_ref[:, cs])
            i = jax.nn.sigmoid(pa[:, G:] + bx_ref[:, cs])
            log_a = (-LRU_C) * r * sp[:, cs]
            a = jnp.exp(log_a)
            th = jnp.tanh(log_a)
            u = jnp.sqrt(-2.0 * th / (1.0 - th)) * (i * xa)

            for s in (1, 2, 4):
                keep = sub >= s
                u = jnp.where(keep, a * pltpu.roll(u, s, 0) + u, u)
                a = jnp.where(keep, a * pltpu.roll(a, s, 0), a)
            a_scr[bi, :, cs] = a
            u_scr[bi, :, cs] = u

    def group(k, hs):
        off = pl.multiple_of(k * SUBLANES, SUBLANES)
        out = []
        for bi in range(nb):
            hg = a_scr[bi, pl.ds(off, SUBLANES), :] * hs[bi] + u_scr[bi, pl.ds(off, SUBLANES), :]
            u_scr[bi, pl.ds(off, SUBLANES), :] = hg
            out.append(hg[SUBLANES - 1:SUBLANES, :])
        return tuple(out)

    h_last = lax.fori_loop(0, rows // SUBLANES, group, tuple(hc_ref[bi] for bi in range(nb)))
    for bi in range(nb):
        hc_ref[bi] = h_last[bi]
        ho_ref[bi] = h_last[bi]
        y_ref[bi] = u_scr[bi] * gg_scr[bi]


def _lru_strided_kernel(x_ref, g_ref, w_ref, cw_ref, cb_ref, wax_ref, ba_ref, bx_ref, lam_ref, conv0_ref, h0_ref,
                        y_ref, convo_ref, ho_ref, tail_ref, hc_ref, *, width):
    W, G = width, LRU_GROUP
    R, nblk = x_ref.shape[0], x_ref.shape[1]

    @pl.when(pl.program_id(1) == 0)
    def _():
        tail_ref[...] = conv0_ref[...]
        hc_ref[...] = h0_ref[...]

    rowi = lax.broadcasted_iota(jnp.int32, (R, G), 0)
    row0 = rowi == 0
    sp = _softplus(-lam_ref[...])
    n = _rms(jnp.concatenate([x_ref[:, j, :] for j in range(nblk)], axis=0), g_ref[...]).astype(bf16)
    blk = lambda a, j: a[j * R:(j + 1) * R]

    for k in range(W // G):
        cs = slice(k * G, (k + 1) * G)
        xl = _dot(n, w_ref[:, cs])
        gg = _gelu_tanh(_dot(n, w_ref[:, W + k * G:W + (k + 1) * G]))
        tails = tail_ref[:, cs]
        cw = cw_ref[:, cs]
        prev_group = {j: jnp.where(row0, tails[j:j + 1, :], pltpu.roll(blk(xl, j), 1, 0))
                      for j in range(nblk - CONV_W + 1, nblk)}
        back = lambda j, s: blk(xl, j - s) if j >= s else prev_group[j - s + nblk]
        xa = jnp.concatenate(
            [cb_ref[:, cs] + cw[CONV_W - 1:CONV_W, :] * blk(xl, j)
             + sum(cw[CONV_W - 1 - s:CONV_W - s, :] * back(j, s) for s in range(1, CONV_W))
             for j in range(nblk)], axis=0)
        new_tail = jnp.concatenate([blk(xl, j)[R - 1:R, :] for j in range(nblk)], axis=0)
        tail_ref[:, cs] = new_tail
        convo_ref[:, cs] = new_tail

        pa = _dot(xa.astype(bf16), wax_ref[k])
        r = jax.nn.sigmoid(pa[:, :G] + ba_ref[:, cs])
        i = jax.nn.sigmoid(pa[:, G:] + bx_ref[:, cs])
        log_a = (-LRU_C) * r * sp[:, cs]
        a = jnp.exp(log_a)
        th = jnp.tanh(log_a)
        u = jnp.sqrt(-2.0 * th / (1.0 - th)) * (i * xa)

        acc_a, acc_u = [blk(a, 0)], [blk(u, 0)]
        for j in range(1, nblk):
            acc_u.append(blk(a, j) * acc_u[-1] + blk(u, j))
            acc_a.append(blk(a, j) * acc_a[-1])
        ga, gu = acc_a[-1], acc_u[-1]
        s = 1
        while s < R:
            keep = rowi >= s
            gu = jnp.where(keep, ga * pltpu.roll(gu, s, 0) + gu, gu)
            ga = jnp.where(keep, ga * pltpu.roll(ga, s, 0), ga)
            s *= 2
        h0 = hc_ref[:, cs]
        h_end = ga * h0 + gu
        h_in = jnp.where(row0, h0, pltpu.roll(h_end, 1, 0))
        hc_ref[:, cs] = h_end[R - 1:R, :]
        ho_ref[:, cs] = h_end[R - 1:R, :]
        for j in range(nblk):
            y_ref[:, j, cs] = (acc_a[j] * h_in + acc_u[j]) * blk(gg, j)


def _lru(x, wts, l, conv8, h0, tl):
    B, L, D = x.shape
    W = conv8.shape[-1]
    nb = SEQ_PER_STEP
    lmap = lambda b, t: (l, 0, 0)
    weights = (wts["mix_norm"], wts["w_all"], wts["lru_conv_w"], wts["lru_conv_b"], wts["lru_wax"], wts["lru_b_a"],
               wts["lru_b_x"], wts["lru_lambda"])
    weight_specs = [
        _resident((None, 1, D), lmap),
        _resident((None, D, 2 * W), lmap),
        _resident((None, CONV_W, W), lmap),
        _resident((None, 1, W), lmap),
        _resident((None, W // LRU_GROUP, LRU_GROUP, 2 * LRU_GROUP), lambda b, t: (l, 0, 0, 0)),
        _resident((None, 1, W), lmap),
        _resident((None, 1, W), lmap),
        _resident((None, 1, W), lmap),
    ]
    if tl % (SUBLANES * SUBLANES) == 0:
        r = tl // SUBLANES
        state = lambda rows: pl.BlockSpec((None, rows, W), lambda b, t: (b, 0, 0))
        y, conv_out, h_out = pl.pallas_call(
            functools.partial(_lru_strided_kernel, width=W),
            grid=(B, L // tl),
            in_specs=[pl.BlockSpec((None, r, SUBLANES, D), lambda b, t: (b, t, 0, 0))] + weight_specs
            + [state(SUBLANES), state(1)],
            out_specs=[pl.BlockSpec((None, r, SUBLANES, W), lambda b, t: (b, t, 0, 0)), state(SUBLANES), state(1)],
            out_shape=[
                jax.ShapeDtypeStruct((B, L // SUBLANES, SUBLANES, W), f32),
                jax.ShapeDtypeStruct((B, SUBLANES, W), f32),
                jax.ShapeDtypeStruct((B, 1, W), f32),
            ],
            scratch_shapes=[pltpu.VMEM((SUBLANES, W), f32), pltpu.VMEM((1, W), f32)],
            compiler_params=pltpu.CompilerParams(dimension_semantics=("arbitrary", "arbitrary"),
                                                 vmem_limit_bytes=VMEM_LIMIT),
            name="lru",
        )(x.reshape(B, L // SUBLANES, SUBLANES, D), *weights, conv8, h0)
        return y.reshape(B, L, W), conv_out, h_out
    return pl.pallas_call(
        functools.partial(_lru_kernel, width=W),
        grid=(B // nb, L // tl),
        in_specs=[
            pl.BlockSpec((nb, tl, D), lambda b, t: (b, t, 0)),
            _resident((None, 1, D), lmap),
            _resident((None, D, 2 * W), lmap),
            _resident((None, CONV_W, W), lmap),
            _resident((None, 1, W), lmap),
            _resident((None, W // LRU_GROUP, LRU_GROUP, 2 * LRU_GROUP), lambda b, t: (l, 0, 0, 0)),
            _resident((None, 1, W), lmap),
            _resident((None, 1, W), lmap),
            _resident((None, 1, W), lmap),
            pl.BlockSpec((nb, SUBLANES, W), lambda b, t: (b, 0, 0)),
            pl.BlockSpec((nb, 1, W), lambda b, t: (b, 0, 0)),
        ],
        out_specs=[
            pl.BlockSpec((nb, tl, W), lambda b, t: (b, t, 0)),
            pl.BlockSpec((nb, SUBLANES, W), lambda b, t: (b, 0, 0)),
            pl.BlockSpec((nb, 1, W), lambda b, t: (b, 0, 0)),
        ],
        out_shape=[
            jax.ShapeDtypeStruct((B, L, W), f32),
            jax.ShapeDtypeStruct((B, SUBLANES, W), f32),
            jax.ShapeDtypeStruct((B, 1, W), f32),
        ],
        scratch_shapes=[
            pltpu.VMEM((nb, SUBLANES, W), f32),
            pltpu.VMEM((nb, 1, W), f32),
            pltpu.VMEM((nb, tl, W), f32),
            pltpu.VMEM((nb, tl, W), f32),
            pltpu.VMEM((nb, tl, W), f32),
        ],
        compiler_params=pltpu.CompilerParams(dimension_semantics=("arbitrary", "arbitrary"),
                                             vmem_limit_bytes=VMEM_LIMIT),
        name="lru",
    )(x, wts["mix_norm"], wts["w_all"], wts["lru_conv_w"], wts["lru_conv_b"], wts["lru_wax"], wts["lru_b_a"],
      wts["lru_b_x"], wts["lru_lambda"], conv8, h0)


def _split3(x):
    a = x.astype(bf16)
    r = x - a.astype(f32)
    b = r.astype(bf16)
    return a, b, (r - b.astype(f32)).astype(bf16)


def _ssd_kernel(x_ref, g_ref, wz_ref, wx_ref, wdt_ref, wdte_ref, wdto_ref, cw_ref, cb_ref, dtb_ref, dtbe_ref, dtbo_ref,
                alog_ref, aloge_ref, alogo_ref, dskip_ref, ng_ref, e3_ref, conv0_ref, h0_ref,
                y_ref, convo_ref, ho_ref,
                tail_ref, hst_ref, y_scr, st_scr, *, q):
    @pl.when(pl.program_id(1) == 0)
    def _():
        tail_ref[...] = conv0_ref[...]
        hst_ref[...] = h0_ref[...]

    rows = x_ref.shape[1]
    sub = min(rows, SSD_SUBTILE)
    for bi in range(x_ref.shape[0]):
        for r0 in range(0, rows, sub):
            rs = pl.ds(r0, sub)
            _ssd_tile(x_ref.at[bi, rs], g_ref, wz_ref, wx_ref, wdt_ref, wdte_ref, wdto_ref, cw_ref, cb_ref, dtb_ref,
                      dtbe_ref, dtbo_ref, alog_ref, aloge_ref, alogo_ref, dskip_ref, ng_ref, e3_ref,
                      y_ref.at[bi, rs], convo_ref.at[bi], ho_ref.at[bi],
                      tail_ref.at[bi], hst_ref.at[bi], y_scr.at[bi], st_scr.at[bi], q=q)


def _ssd_tile(x_ref, g_ref, wz_ref, wx_ref, wdt_ref, wdte_ref, wdto_ref, cw_ref, cb_ref, dtb_ref, dtbe_ref, dtbo_ref,
              alog_ref, aloge_ref, alogo_ref, dskip_ref, ng_ref, e3_ref,
              y_ref, convo_ref, ho_ref, tail_ref, hst_ref, y_scr, st_scr, *, q):
    rows = x_ref.shape[0]
    nch = rows // q
    inner = SSD_HEADS * SSD_HEAD_DIM
    gn = SSD_GROUPS * SSD_STATE
    ppg = SSD_HEADS // 2 // SSD_GROUPS
    gw = inner // SSD_GROUPS

    n = _rms(x_ref[...], g_ref[...]).astype(bf16)

    def conv_cols(cols):
        xr = _dot(n, wx_ref[:, cols])
        convo_ref[:, cols] = xr[rows - SUBLANES:, :]
        return _silu(_causal_conv(tail_ref.at[:, cols], xr, cw_ref[:, cols], cb_ref[:, cols]))

    ri = lax.broadcasted_iota(jnp.int32, (rows, rows), 0)
    ci = lax.broadcasted_iota(jnp.int32, (rows, rows), 1)
    same_blk = (ri // q) == (ci // q)
    tri_bd = (same_blk & (ci <= ri)).astype(f32)
    tri_bd_t = (same_blk & (ri <= ci)).astype(f32)
    li = lax.broadcasted_iota(jnp.int32, (q, 2 * q), 0)
    lj = lax.broadcasted_iota(jnp.int32, (q, 2 * q), 1)
    causal2 = jnp.where(lj >= q, lj - q, lj) <= li
    lane_hi = lax.broadcasted_iota(jnp.int32, (q, LANES), 1) >= SSD_HEAD_DIM
    a_neg = -jnp.exp(alog_ref[...])
    a_neg_e = -jnp.exp(aloge_ref[...])
    a_neg_o = -jnp.exp(alogo_ref[...])

    dt = _softplus(_dot(n, wdt_ref[...]) + dtb_ref[...])
    cum = jnp.dot(tri_bd, dt * a_neg, precision=_HI, preferred_element_type=f32)
    last = jnp.concatenate([jnp.broadcast_to(cum[(c + 1) * q - 1:(c + 1) * q, :], (q, LANES)) for c in range(nch)],
                           axis=0)
    decay_end = jnp.exp(last - cum) * dt
    cum3 = jnp.concatenate(_split3(cum), axis=1)
    dec3 = jnp.concatenate(_split3(decay_end), axis=1)
    dt_e = _softplus(_dot_nt(wdte_ref[...], n) + dtbe_ref[...])
    dt_o = _softplus(_dot_nt(wdto_ref[...], n) + dtbo_ref[...])
    cum_e = jnp.dot(dt_e * a_neg_e, tri_bd_t, precision=_HI, preferred_element_type=f32)
    cum_o = jnp.dot(dt_o * a_neg_o, tri_bd_t, precision=_HI, preferred_element_type=f32)
    chunks = [slice(c * q, (c + 1) * q) for c in range(nch)]
    dt_t2 = [jnp.concatenate([dt_e[:, rc], dt_o[:, rc]], axis=1) for rc in chunks]
    cum_t2 = [jnp.concatenate([cum_e[:, rc], cum_o[:, rc]], axis=1) for rc in chunks]

    for g in range(SSD_GROUPS):
        gs = slice(g * gw, (g + 1) * gw)
        xs = conv_cols(gs)
        bgb = conv_cols(slice(inner + g * SSD_STATE, inner + (g + 1) * SSD_STATE)).astype(bf16)
        cgb = conv_cols(slice(inner + gn + g * SSD_STATE, inner + gn + (g + 1) * SSD_STATE)).astype(bf16)
        cum_x = _dot(cum3, e3_ref[:, gs])
        e_x = jnp.exp(cum_x)
        xd = (xs * _dot(dec3, e3_ref[:, gs])).astype(bf16)
        y_scr[:, gs] = dskip_ref[:, gs] * xs

        for c, rc in enumerate(chunks):
            cb2 = _dot_nt(cgb[rc], jnp.concatenate([bgb[rc], bgb[rc]], axis=0))
            for pr in range(ppg):
                m = g * ppg + pr
                ps = slice(pr * LANES, (pr + 1) * LANES)
                cx = cum_x[rc, ps]
                if 2 * q != LANES:
                    cx = jnp.concatenate([cx[:, :q], cx[:, SSD_HEAD_DIM:SSD_HEAD_DIM + q]], axis=1)
                wm = cb2 * jnp.where(causal2, jnp.exp(cx - cum_t2[c][m:m + 1, :]), 0.0) * dt_t2[c][m:m + 1, :]
                xp = xs[rc, ps]
                x_bd = jnp.concatenate([jnp.where(lane_hi, 0.0, xp), jnp.where(lane_hi, xp, 0.0)], axis=0)
                y_scr[rc, m * LANES:(m + 1) * LANES] += _dot(wm.astype(bf16), x_bd.astype(bf16))
            st_scr[c, :, gs] = _dot_tn(bgb[rc], xd[rc])

        for c, rc in enumerate(chunks):
            y_off = _dot(cgb[rc], hst_ref[:, gs].astype(bf16))
            y_scr[rc, gs] += y_off * e_x[rc]
            hst_ref[:, gs] = e_x[(c + 1) * q - 1:(c + 1) * q, :] * hst_ref[:, gs] + st_scr[c, :, gs]

        y = y_scr[:, gs] * _silu(_dot(n, wz_ref[:, gs]))
        y_ref[:, gs] = _rms(y, ng_ref[:, gs])
    ho_ref[...] = hst_ref[...]


def _ssd(x, wts, l, conv8, h0, tq, q):
    B, L, D = x.shape
    inner = SSD_HEADS * SSD_HEAD_DIM
    cdim = inner + 2 * SSD_GROUPS * SSD_STATE
    half = SSD_HEADS // 2
    nb = SEQ_PER_STEP
    lmap = lambda b, t: (l, 0, 0)
    return pl.pallas_call(
        functools.partial(_ssd_kernel, q=q),
        grid=(B // nb, L // tq),
        in_specs=[
            pl.BlockSpec((nb, tq, D), lambda b, t: (b, t, 0)),
            _resident((None, 1, D), lmap),
            _resident((None, D, inner), lambda b, t: (l, 0, wts["ssd_z_col"] // inner)),
            _resident((None, D, cdim), lambda b, t: (l, 0, wts["ssd_x_col"] // cdim)),
            _resident((None, D, LANES), lmap),
            _resident((None, half, D), lmap),
            _resident((None, half, D), lmap),
            _resident((None, CONV_W, cdim), lmap),
            _resident((None, 1, cdim), lmap),
            _resident((None, 1, LANES), lmap),
            _resident((None, half, 1), lmap),
            _resident((None, half, 1), lmap),
            _resident((None, 1, LANES), lmap),
            _resident((None, half, 1), lmap),
            _resident((None, half, 1), lmap),
            _resident((None, 1, inner), lmap),
            _resident((None, 1, inner), lmap),
            _resident((3 * LANES, inner), lambda b, t: (0, 0)),
            pl.BlockSpec((nb, SUBLANES, cdim), lambda b, t: (b, 0, 0)),
            pl.BlockSpec((nb, SSD_STATE, inner), lambda b, t: (b, 0, 0)),
        ],
        out_specs=[
            pl.BlockSpec((nb, tq, inner), lambda b, t: (b, t, 0)),
            pl.BlockSpec((nb, SUBLANES, cdim), lambda b, t: (b, 0, 0)),
            pl.BlockSpec((nb, SSD_STATE, inner), lambda b, t: (b, 0, 0)),
        ],
        out_shape=[
            jax.ShapeDtypeStruct((B, L, inner), f32),
            jax.ShapeDtypeStruct((B, SUBLANES, cdim), f32),
            jax.ShapeDtypeStruct((B, SSD_STATE, inner), f32),
        ],
        scratch_shapes=[
            pltpu.VMEM((nb, SUBLANES, cdim), f32),
            pltpu.VMEM((nb, SSD_STATE, inner), f32),
            pltpu.VMEM((nb, min(tq, SSD_SUBTILE), inner), f32),
            pltpu.VMEM((nb, min(tq, SSD_SUBTILE) // q, SSD_STATE, inner), f32),
        ],
        compiler_params=pltpu.CompilerParams(dimension_semantics=("arbitrary", "arbitrary"),
                                             vmem_limit_bytes=VMEM_LIMIT),
        name="ssd",
    )(x, wts["mix_norm"], wts["w_all"], wts["w_all"], wts["w_dt"], wts["w_dt_e"], wts["w_dt_o"], wts["ssd_conv_w"],
      wts["ssd_conv_b"], wts["ssd_dt_bias"], wts["ssd_dt_bias_e"], wts["ssd_dt_bias_o"], wts["ssd_a_log"],
      wts["ssd_a_log_e"], wts["ssd_a_log_o"], wts["ssd_d"], wts["ssd_norm"], wts["ssd_expand3"], conv8, h0)


def _attn_kernel(slope_ref, sink_ref, x_ref, g_ref, w_ref, pk_ref, pv_ref,
                 y_ref, ko_ref, vo_ref, kp_scr, vp_scr, *, pos0, sub):
    @pl.when(pl.program_id(1) == 0)
    def _():
        kp_scr[...] = pk_ref[...]
        vp_scr[...] = pv_ref[...]

    for bi in range(x_ref.shape[0]):
        _attn_tile(slope_ref, sink_ref, x_ref.at[bi], g_ref, w_ref, y_ref.at[bi], ko_ref.at[bi], vo_ref.at[bi],
                   kp_scr.at[bi], vp_scr.at[bi], pos0=pos0, sub=sub)


def _attn_tile(slope_ref, sink_ref, x_ref, g_ref, w_ref, y_ref, ko_ref, vo_ref, kp_scr, vp_scr, *, pos0, sub):
    rows = x_ref.shape[0]
    s_len = WINDOW + sub
    aw = N_HEADS * HEAD_DIM
    kw = N_KV_HEADS * HEAD_DIM
    gqa = N_HEADS // N_KV_HEADS
    t = pl.program_id(1)

    n = _rms(x_ref[...], g_ref[...]).astype(bf16)
    p = _dot(n, w_ref[...])
    qq = p[:, :aw] * (HEAD_DIM ** -0.5)
    kk = jnp.concatenate([kp_scr[...], p[:, aw:aw + kw]], axis=0)
    vv = jnp.concatenate([vp_scr[...], p[:, aw + kw:aw + 2 * kw]], axis=0)
    k_new = kk[rows:, :]
    v_new = vv[rows:, :]
    kp_scr[...] = k_new
    vp_scr[...] = v_new
    ko_ref[...] = k_new
    vo_ref[...] = v_new

    iq = lax.broadcasted_iota(jnp.int32, (sub, s_len), 0)
    ik = lax.broadcasted_iota(jnp.int32, (sub, s_len), 1)
    neg_dist = -jnp.abs(iq - (ik - WINDOW)).astype(f32)
    qc = iq // CHUNK
    kc = ik // CHUNK - WIN_CHUNKS
    in_band = (kc >= qc - WIN_CHUNKS) & (kc <= qc)
    lane_q = lax.broadcasted_iota(jnp.int32, (sub, LANES), 1) >= HEAD_DIM
    lane_k = lax.broadcasted_iota(jnp.int32, (WINDOW + rows, LANES), 1) >= HEAD_DIM

    k_both, v_lo, v_hi = [], [], []
    for kvh in range(N_KV_HEADS):
        tile = slice((kvh // 2) * LANES, (kvh // 2 + 1) * LANES)
        odd = kvh % 2 == 1
        k_own = jnp.where(lane_k, kk[:, tile], 0.0) if odd else jnp.where(lane_k, 0.0, kk[:, tile])
        v_own = jnp.where(lane_k, vv[:, tile], 0.0) if odd else jnp.where(lane_k, 0.0, vv[:, tile])
        v_other = pltpu.roll(v_own, HEAD_DIM, 1)
        k_both.append((k_own + pltpu.roll(k_own, HEAD_DIM, 1)).astype(bf16))
        v_lo.append((v_other if odd else v_own).astype(bf16))
        v_hi.append((v_own if odd else v_other).astype(bf16))

    for st in range(rows // sub):
        r0 = st * sub
        k_pos = ik + (pos0 - WINDOW + r0 + t * rows)
        bias_base = jnp.where(in_band & (k_pos >= 0), neg_dist, -MASK_DIST)
        for kvh in range(N_KV_HEADS):
            if gqa * sub > ATT_STACK_ROWS:
                for m in range(kvh * gqa // 2, (kvh + 1) * gqa // 2):
                    qp = qq[r0:r0 + sub, m * LANES:(m + 1) * LANES]
                    o_pair = jnp.zeros((sub, LANES), f32)
                    for half in range(2):
                        h = 2 * m + half
                        qh = (jnp.where(lane_q, qp, 0.0) if half else jnp.where(lane_q, 0.0, qp)).astype(bf16)
                        s = _dot_nt(qh, k_both[kvh][r0:r0 + s_len]) + slope_ref[h] * bias_base
                        mx = jnp.maximum(jnp.max(s, axis=-1, keepdims=True), sink_ref[h])
                        e = jnp.exp(s - mx)
                        den = jnp.sum(e, axis=-1, keepdims=True) + jnp.exp(sink_ref[h] - mx)
                        vh = (v_hi if half else v_lo)[kvh][r0:r0 + s_len]
                        o_pair = o_pair + _dot(e.astype(bf16), vh) * (1.0 / den)
                    y_ref[r0:r0 + sub, m * LANES:(m + 1) * LANES] = o_pair
                continue
            q_rows, bias_rows, sink_rows = [], [], []
            for j in range(gqa):
                h = kvh * gqa + j
                qp = qq[r0:r0 + sub, (h // 2) * LANES:(h // 2 + 1) * LANES]
                q_rows.append((jnp.where(lane_q, qp, 0.0) if h % 2 else jnp.where(lane_q, 0.0, qp)).astype(bf16))
                bias_rows.append(slope_ref[h] * bias_base)
                sink_rows.append(jnp.full((sub, 1), sink_ref[h], f32))
            sink_col = jnp.concatenate(sink_rows, axis=0)
            s = (_dot_nt(jnp.concatenate(q_rows, axis=0), k_both[kvh][r0:r0 + s_len])
                 + jnp.concatenate(bias_rows, axis=0))
            mx = jnp.maximum(jnp.max(s, axis=-1, keepdims=True), sink_col)
            e = jnp.exp(s - mx)
            inv = 1.0 / (jnp.sum(e, axis=-1, keepdims=True) + jnp.exp(sink_col - mx))
            eb = e.astype(bf16)
            blk = lambda a, j: a[j * sub:(j + 1) * sub]
            even = [j for j in range(gqa) if (kvh * gqa + j) % 2 == 0]
            odd = [j for j in range(gqa) if (kvh * gqa + j) % 2 == 1]
            o_lo = _dot(jnp.concatenate([blk(eb, j) for j in even], axis=0), v_lo[kvh][r0:r0 + s_len])
            o_hi = _dot(jnp.concatenate([blk(eb, j) for j in odd], axis=0), v_hi[kvh][r0:r0 + s_len])
            for i, (je, jo) in enumerate(zip(even, odd)):
                m = (kvh * gqa + je) // 2
                y_ref[r0:r0 + sub, m * LANES:(m + 1) * LANES] = (blk(o_lo, i) * blk(inv, je)
                                                                 + blk(o_hi, i) * blk(inv, jo))


def _attn(x, wts, l, slopes, sink, prev_k, prev_v, tq, pos0):
    B, L, D = x.shape
    aw = N_HEADS * HEAD_DIM
    kw = N_KV_HEADS * HEAD_DIM
    nb = SEQ_PER_STEP
    lmap = lambda b, t: (l, 0, 0)
    smem = pl.BlockSpec(memory_space=pltpu.SMEM)
    return pl.pallas_call(
        functools.partial(_attn_kernel, pos0=pos0, sub=min(tq, ATT_SUB)),
        grid=(B // nb, L // tq),
        in_specs=[
            smem, smem,
            pl.BlockSpec((nb, tq, D), lambda b, t: (b, t, 0)),
            _resident((None, 1, D), lmap),
            _resident((None, D, aw + 2 * kw), lmap),
            pl.BlockSpec((nb, WINDOW, kw), lambda b, t: (b, 0, 0)),
            pl.BlockSpec((nb, WINDOW, kw), lambda b, t: (b, 0, 0)),
        ],
        out_specs=[
            pl.BlockSpec((nb, tq, aw), lambda b, t: (b, t, 0)),
            pl.BlockSpec((nb, WINDOW, kw), lambda b, t: (b, 0, 0)),
            pl.BlockSpec((nb, WINDOW, kw), lambda b, t: (b, 0, 0)),
        ],
        out_shape=[
            jax.ShapeDtypeStruct((B, L, aw), f32),
            jax.ShapeDtypeStruct((B, WINDOW, kw), f32),
            jax.ShapeDtypeStruct((B, WINDOW, kw), f32),
        ],
        scratch_shapes=[pltpu.VMEM((nb, WINDOW, kw), f32), pltpu.VMEM((nb, WINDOW, kw), f32)],
        compiler_params=pltpu.CompilerParams(dimension_semantics=("arbitrary", "arbitrary"),
                                             vmem_limit_bytes=VMEM_LIMIT),
        name="attn",
    )(slopes, sink, x, wts["mix_norm"], wts["w_qkv"], prev_k, prev_v)


def _merge_kernel(x_ref, ya_ref, yb_ref, yc_ref, g_ref, wg_ref, wb_ref, wo_ref, o_ref):
    x = x_ref[...]
    D = x.shape[-1]
    n = _rms(x, g_ref[...]).astype(bf16)
    merged = jnp.zeros(x.shape, f32)
    for b, y_ref in enumerate((ya_ref, yb_ref, yc_ref)):
        bw = y_ref.shape[-1]
        gate = jax.nn.sigmoid(_dot(n, wg_ref[:, b * D:(b + 1) * D]))
        merged = merged + gate * _dot(y_ref[...].astype(bf16), wb_ref[b * bw:(b + 1) * bw, :])
    o_ref[...] = x + _dot(merged.astype(bf16), wo_ref[...])


def _merge(x, ya, yb, yc, wts, l):
    T, D = x.shape
    tm = min(T, 512)
    lmap = lambda i: (l, 0, 0)
    tok = lambda w: pl.BlockSpec((tm, w), lambda i: (i, 0))
    return pl.pallas_call(
        _merge_kernel,
        grid=(T // tm,),
        in_specs=[
            tok(D), tok(ya.shape[-1]), tok(yb.shape[-1]), tok(yc.shape[-1]),
            _resident((None, 1, D), lmap),
            _resident((None, D, 3 * D), lmap),
            _resident((None, wts["w_branch"].shape[1], D), lmap),
            _resident((None, D, D), lmap),
        ],
        out_specs=tok(D),
        out_shape=jax.ShapeDtypeStruct((T, D), f32),
        compiler_params=pltpu.CompilerParams(dimension_semantics=("parallel",), vmem_limit_bytes=VMEM_LIMIT),
        name="merge",
    )(x, ya, yb, yc, wts["mix_norm"], wts["w_gates"], wts["w_branch"], wts["w_out"])


def _cast_kernel(w_ref, all_ref, qkv_ref, gates_ref, *, qkv_cols, gates_cols):
    w = w_ref[...]
    all_ref[...] = w.astype(bf16)
    qkv_ref[...] = w[:, qkv_cols[0]:qkv_cols[1]].astype(bf16)
    gates_ref[...] = w[:, gates_cols[0]:gates_cols[1]].astype(bf16)


def _cast_w_in(w_in, qkv_cols, gates_cols):
    nl, D, N = w_in.shape
    tr = 256
    spec = lambda w: pl.BlockSpec((None, tr, w), lambda l, i: (l, i, 0))
    widths = (N, qkv_cols[1] - qkv_cols[0], gates_cols[1] - gates_cols[0])
    return pl.pallas_call(
        functools.partial(_cast_kernel, qkv_cols=qkv_cols, gates_cols=gates_cols),
        grid=(nl, D // tr),
        in_specs=[spec(N)],
        out_specs=[spec(w) for w in widths],
        out_shape=[jax.ShapeDtypeStruct((nl, D, w), bf16) for w in widths],
        compiler_params=pltpu.CompilerParams(dimension_semantics=("parallel", "parallel"),
                                             vmem_limit_bytes=VMEM_LIMIT),
        name="cast_w_in",
    )(w_in)


def _block_diag_groups(w):
    nl, nb, bs, _ = w.shape
    per = LRU_GROUP // bs
    w = w.reshape(nl, nb // per, per, bs, bs)
    eye = jnp.eye(per, dtype=w.dtype)
    out = w[:, :, :, :, None, :] * eye[None, None, :, None, :, None]
    return out.reshape(nl, nb // per, LRU_GROUP, LRU_GROUP)


def _prep_weights(ffn_norm, ffn_w_gate, ffn_w_up, ffn_w_down, mix_norm, w_in, lru_conv_w, lru_conv_b, lru_w_a,
                  lru_b_a, lru_w_x, lru_b_x, lru_lambda, ssd_conv_w, ssd_conv_b, ssd_dt_bias, ssd_a_log, ssd_d,
                  ssd_norm, w_branch, w_out):
    nl, D = mix_norm.shape
    W = lru_conv_w.shape[-1]
    inner = SSD_HEADS * SSD_HEAD_DIM
    cdim = ssd_conv_w.shape[-1]
    aw = N_HEADS * HEAD_DIM
    kw = N_KV_HEADS * HEAD_DIM
    o = [0]
    for s in (W, W, inner, cdim, SSD_HEADS, aw, kw, kw, 3 * D):
        o.append(o[-1] + s)
    row = lambda a: a.reshape(nl, 1, -1)
    pad_lanes = lambda a: jnp.pad(a, [(0, 0)] * (a.ndim - 1) + [(0, LANES - a.shape[-1])])
    assert o[0] == 0 and o[2] % inner == 0 and o[3] % cdim == 0, "column ranges must be whole blocks"
    w_all, w_qkv, w_gates = _cast_w_in(w_in, (o[5], o[8]), (o[8], o[9]))
    w_dt = w_in[:, :, o[4]:o[5]]
    w_dt_t = jnp.swapaxes(w_dt, 1, 2).astype(bf16)
    col = lambda a: a[:, :, None]
    return {
        "ffn_norm": ffn_norm.reshape(nl, 2, 1, D),
        "ffn_w_gate": ffn_w_gate.astype(bf16), "ffn_w_up": ffn_w_up.astype(bf16), "ffn_w_down": ffn_w_down.astype(bf16),
        "mix_norm": row(mix_norm),
        "w_all": w_all, "ssd_z_col": o[2], "ssd_x_col": o[3],
        "w_dt": pad_lanes(w_dt).astype(bf16),
        "w_dt_e": w_dt_t[:, 0::2], "w_dt_o": w_dt_t[:, 1::2],
        "w_qkv": w_qkv, "w_gates": w_gates,
        "lru_conv_w": lru_conv_w, "lru_conv_b": row(lru_conv_b),
        "lru_wax": jnp.concatenate([_block_diag_groups(lru_w_a), _block_diag_groups(lru_w_x)], axis=-1).astype(bf16),
        "lru_b_a": row(lru_b_a), "lru_b_x": row(lru_b_x), "lru_lambda": row(lru_lambda),
        "ssd_conv_w": ssd_conv_w, "ssd_conv_b": row(ssd_conv_b),
        "ssd_dt_bias": pad_lanes(row(ssd_dt_bias)),
        "ssd_dt_bias_e": col(ssd_dt_bias[:, 0::2]), "ssd_dt_bias_o": col(ssd_dt_bias[:, 1::2]),
        "ssd_a_log": pad_lanes(row(ssd_a_log)),
        "ssd_a_log_e": col(ssd_a_log[:, 0::2]), "ssd_a_log_o": col(ssd_a_log[:, 1::2]),
        "ssd_d": row(jnp.repeat(ssd_d, SSD_HEAD_DIM, axis=-1)),
        "ssd_norm": row(ssd_norm),
        "ssd_expand3": (jnp.arange(3 * LANES)[:, None] % LANES
                        == (jnp.arange(inner) // SSD_HEAD_DIM)[None, :]).astype(bf16),
        "w_branch": w_branch.astype(bf16), "w_out": w_out.astype(bf16),
    }


def _pad_conv_state(s):
    return jnp.pad(s, ((0, 0), (SUBLANES - s.shape[1], 0), (0, 0)))


def _layer(x, wts, l, slopes, sink, state, tiles, pos0, final_norm):
    B, L, D = x.shape
    cache_k, cache_v, lru_conv, lru_h, ssd_conv, ssd_h = state
    tl, tq_ssd, q, tq_att = tiles
    kw = N_KV_HEADS * HEAD_DIM
    inner = SSD_HEADS * SSD_HEAD_DIM
    ffn_w = (wts["ffn_norm"], wts["ffn_w_gate"], wts["ffn_w_up"], wts["ffn_w_down"])

    x = _ffn(x.reshape(B * L, D), *ffn_w, l, 0).reshape(B, L, D)
    ya, lru_conv8, lru_hn = _lru(x, wts, l, _pad_conv_state(lru_conv), lru_h[:, None, :], tl)
    ssd_ht = jnp.swapaxes(ssd_h.reshape(B, inner, SSD_STATE), 1, 2)
    yb, ssd_conv8, ssd_htn = _ssd(x, wts, l, _pad_conv_state(ssd_conv), ssd_ht, tq_ssd, q)
    ssd_hn = jnp.swapaxes(ssd_htn, 1, 2)
    yc, kn, vn = _attn(x, wts, l, slopes, sink, cache_k.reshape(B, WINDOW, kw), cache_v.reshape(B, WINDOW, kw),
                       tq_att, pos0)
    flat = lambda a: a.reshape(B * L, a.shape[-1])
    x = _merge(flat(x), flat(ya), flat(yb), flat(yc), wts, l)
    x = _ffn(x, *ffn_w, l, 1, final_norm=final_norm).reshape(B, L, D)
    new_state = (kn.reshape(B, WINDOW, N_KV_HEADS, HEAD_DIM), vn.reshape(B, WINDOW, N_KV_HEADS, HEAD_DIM),
                 lru_conv8[:, SUBLANES - (CONV_W - 1):], lru_hn[:, 0],
                 ssd_conv8[:, SUBLANES - (CONV_W - 1):], ssd_hn.reshape(B, SSD_HEADS, SSD_HEAD_DIM, SSD_STATE))
    return x, new_state


def kernel(x_prompt, x_sample, cache_attn_k, cache_attn_v, state_lru_conv, state_lru_h, state_ssd_conv, state_ssd_h, ffn_norm, ffn_w_gate, ffn_w_up, ffn_w_down, mix_norm, w_in, lru_conv_w, lru_conv_b, lru_w_a, lru_b_a, lru_w_x, lru_b_x, lru_lambda, ssd_conv_w, ssd_conv_b, ssd_dt_bias, ssd_a_log, ssd_d, ssd_norm, attn_sink, w_branch, w_out, final_norm):
    depth = mix_norm.shape[0]
    bp, lp, D = x_prompt.shape
    bs, ls, _ = x_sample.shape
    wts = _prep_weights(ffn_norm, ffn_w_gate, ffn_w_up, ffn_w_down, mix_norm, w_in, lru_conv_w, lru_conv_b, lru_w_a,
                        lru_b_a, lru_w_x, lru_b_x, lru_lambda, ssd_conv_w, ssd_conv_b, ssd_dt_bias, ssd_a_log, ssd_d,
                        ssd_norm, w_branch, w_out)
    slopes = jnp.exp2(-8.0 * (jnp.arange(N_HEADS, dtype=f32) + 1.0) / N_HEADS)
    fin = final_norm.reshape(1, D)
    kw = N_KV_HEADS * HEAD_DIM
    zero_state = (jnp.zeros((bp, WINDOW, kw), f32), jnp.zeros((bp, WINDOW, kw), f32),
                  jnp.zeros((bp, CONV_W - 1, lru_conv_w.shape[-1]), f32), jnp.zeros((bp, lru_conv_w.shape[-1]), f32),
                  jnp.zeros((bp, CONV_W - 1, ssd_conv_w.shape[-1]), f32),
                  jnp.zeros((bp, SSD_HEADS, SSD_HEAD_DIM, SSD_STATE), f32))
    tiles_p = (min(lp, 512), min(lp, 512), CHUNK, min(lp, 512))
    tiles_s = (ls, ls, ls, ls)
    xp, xs = x_prompt, x_sample
    p_new = [[] for _ in range(6)]
    s_new = [[] for _ in range(6)]
    for l in range(depth):
        last = fin if l == depth - 1 else None
        xp, st_p = _layer(xp, wts, l, slopes, attn_sink[l], zero_state, tiles_p, 0, last)
        st_s = (cache_attn_k[l], cache_attn_v[l], state_lru_conv[l], state_lru_h[l], state_ssd_conv[l], state_ssd_h[l])
        xs, st_sn = _layer(xs, wts, l, slopes, attn_sink[l], st_s, tiles_s, PAST_LEN, last)
        for j in range(6):
            p_new[j].append(st_p[j])
            s_new[j].append(st_sn[j])
    pk, pv, plc, plh, psc, psh = [jnp.stack(t, axis=0) for t in p_new]
    sk, sv, slc, slh, ssc, ssh = [jnp.stack(t, axis=0) for t in s_new]
    return (xp, xs, pk, pv, plc, plh, psc, psh, sk, sv, slc, slh, ssc, ssh)
```
